```python
import jax, jax.numpy as jnp
from jax import lax
import numpy as np

D_MODEL = 1024
BATCH = 32
SEQ = 2048
DEPTH = 2

MEM_LEN = 256
BRANCH_WIDTH = D_MODEL // 2
N_BRANCH = 4

RWKV_HEAD_DIM = 64
RWKV_HEADS = BRANCH_WIDTH // RWKV_HEAD_DIM
RWKV_LORA = 64
RWKV_GN_EPS = 64e-5

MLA_NOPE = 64
MLA_ROPE = 32
MLA_V = 64
MLA_HEADS = BRANCH_WIDTH // MLA_V
MLA_Q_LORA = 3 * D_MODEL // 8
MLA_KV_LORA = D_MODEL // 4
ROPE_THETA = 10000.0
Q_BLOCK = 128

CONV_KERNEL = 31

XATTN_HEADS = 4
XATTN_HEAD_DIM = BRANCH_WIDTH // XATTN_HEADS

DEEPNORM_ALPHA = (2.0 * DEPTH) ** 0.25
DEEPNORM_BETA = (8.0 * DEPTH) ** -0.25
LN_EPS = 1e-5
RMS_EPS = 1e-6
MASK_VALUE = -1e30

RWKV_SHIFT_COLS = 3 * BRANCH_WIDTH + 2 * RWKV_LORA
SPLIT_SIZES = (
    RWKV_SHIFT_COLS, BRANCH_WIDTH,
    MLA_Q_LORA, MLA_KV_LORA, MLA_ROPE, BRANCH_WIDTH,
    2 * BRANCH_WIDTH, BRANCH_WIDTH,
    BRANCH_WIDTH, BRANCH_WIDTH,
    N_BRANCH * D_MODEL,
)
SPLIT_POINTS = tuple(int(s) for s in np.cumsum(SPLIT_SIZES)[:-1])
IN_COLS = int(sum(SPLIT_SIZES))

kernel_name = "hybrid_rwkv7_mla_conformer_xattn_deepnorm"


def _layer_norm(x, g, b):
    xf = x.astype(jnp.float32)
    mu = jnp.mean(xf, -1, keepdims=True)
    var = jnp.mean(jnp.square(xf - mu), -1, keepdims=True)
    return ((xf - mu) * lax.rsqrt(var + LN_EPS)).astype(x.dtype) * g + b


def _rms_norm(x, g):
    xf = x.astype(jnp.float32)
    y = xf * lax.rsqrt(jnp.mean(jnp.square(xf), -1, keepdims=True) + RMS_EPS)
    return y.astype(x.dtype) * g


def _rope(x, cos, sin):
    half = x.shape[-1] // 2
    x1, x2 = x[..., :half], x[..., half:]
    return jnp.concatenate([x1 * cos - x2 * sin, x1 * sin + x2 * cos], axis=-1)


def _rwkv7_branch(p, gate, mu, w0, w2, a0, a2, k_k, k_a, r_k, lnx_g, lnx_b):
    B, S, _ = p.shape
    H, N, W = RWKV_HEADS, RWKV_HEAD_DIM, BRANCH_WIDTH
    p_prev = jnp.pad(p, ((0, 0), (1, 0), (0, 0)))[:, :-1]
    p = p + mu * (p_prev - p)
    r, k, v, wl, al = jnp.split(p, (W, 2 * W, 3 * W, 3 * W + RWKV_LORA), axis=-1)
    w_log = -jax.nn.softplus(-(w0 + jnp.tanh(wl) @ w2)) - 0.5
    decay = jnp.exp(-jnp.exp(w_log.astype(jnp.float32)))
    a = jax.nn.sigmoid(a0 + al @ a2)

    def heads(t):
        return t.reshape(B, S, H, N).astype(jnp.float32)

    kk = heads(k * k_k)
    kk = kk / jnp.maximum(jnp.linalg.norm(kk, axis=-1, keepdims=True), 1e-12)
    k = k * (1.0 + (a - 1.0) * k_a)
    r_h, k_h, v_h, a_h, w_h = heads(r), heads(k), heads(v), heads(a), heads(decay)
    xs = tuple(jnp.moveaxis(t, 1, 0) for t in (r_h, w_h, k_h, v_h, -kk, kk * a_h))

    def step(state, inp):
        r_t, w_t, k_t, v_t, a_t, b_t = inp
        sa = jnp.einsum('bhvk,bhk->bhv', state, a_t)
        state = (state * w_t[:, :, None, :] + sa[..., None] * b_t[:, :, None, :]
                 + v_t[..., None] * k_t[:, :, None, :])
        return state, jnp.einsum('bhvk,bhk->bhv', state, r_t)

    s0 = jnp.zeros((B, H, N, N), jnp.float32)
    _, y = lax.scan(step, s0, xs)
    y = jnp.moveaxis(y, 0, 1)
    y_mu = jnp.mean(y, -1, keepdims=True)
    y_var = jnp.mean(jnp.square(y - y_mu), -1, keepdims=True)
    y = ((y - y_mu) * lax.rsqrt(y_var + RWKV_GN_EPS)).reshape(B, S, W) * lnx_g + lnx_b
    bonus = jnp.sum(r_h * k_h * r_k, axis=-1, keepdims=True) * v_h
    y = y + bonus.reshape(B, S, W)
    return y.astype(gate.dtype) * jax.nn.silu(gate)


def _causal_block_attention(q, k, v, scale):
    B, S, H, Dq = q.shape
    nb = S // Q_BLOCK
    qb = jnp.moveaxis(q.reshape(B, nb, Q_BLOCK, H, Dq), 1, 0)
    key_idx = jnp.arange(S)

    def one_block(args):
        q_blk, i = args
        s = jnp.einsum('bqhd,bkhd->bhqk', q_blk, k).astype(jnp.float32) * scale
        q_idx = i * Q_BLOCK + jnp.arange(Q_BLOCK)
        s = jnp.where(key_idx[None, :] <= q_idx[:, None], s, MASK_VALUE)
        pr = jax.nn.softmax(s, axis=-1).astype(v.dtype)
        return jnp.einsum('bhqk,bkhd->bqhd', pr, v)

    o = lax.map(one_block, (qb, jnp.arange(nb)))
    return jnp.moveaxis(o, 0, 1).reshape(B, S, H, v.shape[-1])


def _mla_branch(q_lat, kv_lat, k_pe, gate, cos, sin, q_norm, w_uq, kv_norm, w_ukv):
    B, S, _ = q_lat.shape
    H = MLA_HEADS
    q = (_rms_norm(q_lat, q_norm) @ w_uq).reshape(B, S, H, MLA_NOPE + MLA_ROPE)
    kv = (_rms_norm(kv_lat, kv_norm) @ w_ukv).reshape(B, S, H, MLA_NOPE + MLA_V)
    q_nope, q_pe = q[..., :MLA_NOPE], q[..., MLA_NOPE:]
    k_nope, v = kv[..., :MLA_NOPE], kv[..., MLA_NOPE:]
    q_pe = _rope(q_pe, cos[:, :, None, :], sin[:, :, None, :])
    k_pe = _rope(k_pe, cos, sin)
    q_full = jnp.concatenate([q_nope, q_pe], axis=-1)
    k_full = jnp.concatenate(
        [k_nope, jnp.broadcast_to(k_pe[:, :, None, :], (B, S, H, MLA_ROPE))], axis=-1)
    o = _causal_block_attention(q_full, k_full, v, (MLA_NOPE + MLA_ROPE) ** -0.5)
    return o.reshape(B, S, BRANCH_WIDTH) * jax.nn.silu(gate)


def _conformer_conv_branch(u, gate, conv_w, conv_b, ln_g, ln_b):
    val, glu_gate = jnp.split(u, 2, axis=-1)
    h = val * jax.nn.sigmoid(glu_gate)
    h = lax.conv_general_dilated(
        h, conv_w[:, None, :], window_strides=(1,), padding=[(CONV_KERNEL - 1, 0)],
        dimension_numbers=('NWC', 'WIO', 'NWC'), feature_group_count=BRANCH_WIDTH) + conv_b
    h = jax.nn.silu(_layer_norm(h, ln_g, ln_b))
    return h * jax.nn.silu(gate)


def _memory_xattn_branch(q, gate, mem, w_mem_kv):
    B, S, _ = q.shape
    M = mem.shape[1]
    kv = mem @ w_mem_kv
    k = kv[..., :BRANCH_WIDTH].reshape(B, M, XATTN_HEADS, XATTN_HEAD_DIM)
    v = kv[..., BRANCH_WIDTH:].reshape(B, M, XATTN_HEADS, XATTN_HEAD_DIM)
    qh = q.reshape(B, S, XATTN_HEADS, XATTN_HEAD_DIM)
    s = jnp.einsum('bshd,bmhd->bhsm', qh, k).astype(jnp.float32) * XATTN_HEAD_DIM ** -0.5
    pr = jax.nn.softmax(s, axis=-1).astype(v.dtype)
    o = jnp.einsum('bhsm,bmhd->bshd', pr, v).reshape(B, S, BRANCH_WIDTH)
    return o * jax.nn.silu(gate)


def setup_inputs(seed: int = 0) -> dict:
    key = jax.random.key(seed)
    ks = jax.random.split(key, 32)
    f32 = jnp.float32
    L, D, W = DEPTH, D_MODEL, BRANCH_WIDTH

    def nrm(k, shape, scale):
        return jax.random.normal(k, shape, f32) * scale

    x = jax.random.normal(ks[0], (BATCH, SEQ, D), f32)
    mem = jax.random.normal(ks[1], (BATCH, MEM_LEN, D), f32)
    start = jax.random.randint(ks[2], (BATCH, 1), 0, 4096, dtype=jnp.int32)
    positions = start + jnp.arange(SEQ, dtype=jnp.int32)[None, :]
    return {
        "x": x,
        "mem": mem,
        "positions": positions,
        "w_in": nrm(ks[3], (L, D, IN_COLS), D ** -0.5),
        "b_gate": nrm(ks[4], (L, N_BRANCH, D), 0.1),
        "rwkv_mu": jax.random.uniform(ks[5], (L, RWKV_SHIFT_COLS), f32),
        "rwkv_w0": jax.random.uniform(ks[6], (L, W), f32, -6.0, 0.0),
        "rwkv_w2": nrm(ks[7], (L, RWKV_LORA, W), 0.5 * RWKV_LORA ** -0.5),
        "rwkv_a0": nrm(ks[8], (L, W), 0.1),
        "rwkv_a2": nrm(ks[9], (L, RWKV_LORA, W), 0.5 * RWKV_LORA ** -0.5),
        "rwkv_k_k": 0.85 + nrm(ks[10], (L, W), 0.05),
        "rwkv_k_a": 1.0 + nrm(ks[11], (L, W), 0.05),
        "rwkv_r_k": nrm(ks[12], (L, RWKV_HEADS, RWKV_HEAD_DIM), 0.3),
        "rwkv_lnx_g": 1.0 + nrm(ks[13], (L, W), 0.02),
        "rwkv_lnx_b": nrm(ks[14], (L, W), 0.02),
        "mla_q_norm": 1.0 + nrm(ks[15], (L, MLA_Q_LORA), 0.02),
        "mla_w_uq": nrm(ks[16], (L, MLA_Q_LORA, MLA_HEADS * (MLA_NOPE + MLA_ROPE)), MLA_Q_LORA ** -0.5),
        "mla_kv_norm": 1.0 + nrm(ks[17], (L, MLA_KV_LORA), 0.02),
        "mla_w_ukv": nrm(ks[18], (L, MLA_KV_LORA, MLA_HEADS * (MLA_NOPE + MLA_V)), MLA_KV_LORA ** -0.5),
        "conv_w": nrm(ks[19], (L, CONV_KERNEL, W), CONV_KERNEL ** -0.5),
        "conv_b": nrm(ks[20], (L, W), 0.02),
        "conv_ln_g": 1.0 + nrm(ks[21], (L, W), 0.02),
        "conv_ln_b": nrm(ks[22], (L, W), 0.02),
        "xattn_w_mem_kv": nrm(ks[23], (L, D, 2 * W), D ** -0.5),
        "w_o_branch": nrm(ks[24], (L, N_BRANCH, W, D), DEEPNORM_BETA * W ** -0.5),
        "w_out": nrm(ks[25], (L, D, D), DEEPNORM_BETA * D ** -0.5),
        "ln_g": 1.0 + nrm(ks[26], (L, D), 0.02),
        "ln_b": nrm(ks[27], (L, D), 0.02),
    }


def reference(x, mem, positions, w_in, b_gate, rwkv_mu, rwkv_w0, rwkv_w2, rwkv_a0, rwkv_a2,
              rwkv_k_k, rwkv_k_a, rwkv_r_k, rwkv_lnx_g, rwkv_lnx_b, mla_q_norm, mla_w_uq,
              mla_kv_norm, mla_w_ukv, conv_w, conv_b, conv_ln_g, conv_ln_b, xattn_w_mem_kv,
              w_o_branch, w_out, ln_g, ln_b):
    B, S, D = x.shape
    inv_freq = ROPE_THETA ** (-jnp.arange(0, MLA_ROPE, 2, dtype=jnp.float32) / MLA_ROPE)
    ang = positions.astype(jnp.float32)[..., None] * inv_freq
    cos = jnp.cos(ang).astype(x.dtype)
    sin = jnp.sin(ang).astype(x.dtype)
    for l in range(DEPTH):
        h = x @ w_in[l]
        (rw_p, rw_g, q_lat, kv_lat, k_pe, mla_g, conv_u, conv_g, xq, xg,
         merge) = jnp.split(h, SPLIT_POINTS, axis=-1)
        y_rwkv = _rwkv7_branch(rw_p, rw_g, rwkv_mu[l], rwkv_w0[l], rwkv_w2[l], rwkv_a0[l],
                               rwkv_a2[l], rwkv_k_k[l], rwkv_k_a[l], rwkv_r_k[l],
                               rwkv_lnx_g[l], rwkv_lnx_b[l])
        y_mla = _mla_branch(q_lat, kv_lat, k_pe, mla_g, cos, sin, mla_q_norm[l], mla_w_uq[l],
                            mla_kv_norm[l], mla_w_ukv[l])
        y_conv = _conformer_conv_branch(conv_u, conv_g, conv_w[l], conv_b[l], conv_ln_g[l],
                                        conv_ln_b[l])
        y_mem = _memory_xattn_branch(xq, xg, mem, xattn_w_mem_kv[l])
        branches = jnp.stack([y_rwkv, y_mla, y_conv, y_mem], axis=2)
        proj = jnp.einsum('bsnc,ncd->bsnd', branches, w_o_branch[l])
        gates = jax.nn.sigmoid(merge.reshape(B, S, N_BRANCH, D) + b_gate[l])
        merged = jnp.sum(gates * proj, axis=2)
        out = merged @ w_out[l]
        x = _layer_norm(DEEPNORM_ALPHA * x + out, ln_g[l], ln_b[l])
    return x
```

```python
import functools

import jax
import jax.numpy as jnp
from jax import lax
from jax.experimental import pallas as pl
from jax.experimental.pallas import tpu as pltpu

F32 = jnp.float32
BF16 = jnp.bfloat16

N_BRANCH = 4
RWKV_HEAD_DIM = 64
RWKV_LORA = 64
RWKV_GN_EPS = 64e-5
MLA_NOPE = 64
MLA_ROPE = 32
MLA_V = 64
ROPE_THETA = 10000.0
CONV_KERNEL = 31
XATTN_HEADS = 4
LN_EPS = 1e-5
RMS_EPS = 1e-6

LANE = 128
CHUNK = 64
CONV_HALO = 32
VMEM_LIMIT = 56 * 1024 * 1024


def _dot(a, b):
    return jnp.dot(a.astype(BF16), b.astype(BF16), preferred_element_type=F32)


def _dot_nt(a, b):
    return lax.dot_general(a.astype(BF16), b.astype(BF16), (((1,), (1,)), ((), ())),
                           preferred_element_type=F32)


def _dot_tn(a, b):
    return lax.dot_general(a.astype(BF16), b.astype(BF16), (((0,), (0,)), ((), ())),
                           preferred_element_type=F32)


def _split_hi_lo(a):
    hi = a.astype(BF16)
    lo = (a - hi.astype(F32)).astype(BF16)
    return hi, lo


def _dot_0_1_rhs(a, ones_like_matrix):
    hi, lo = _split_hi_lo(a)
    return (jnp.dot(hi, ones_like_matrix, preferred_element_type=F32)
            + jnp.dot(lo, ones_like_matrix, preferred_element_type=F32))


def _dot_0_1_lhs(ones_like_matrix, a):
    hi, lo = _split_hi_lo(a)
    return (jnp.dot(ones_like_matrix, hi, preferred_element_type=F32)
            + jnp.dot(ones_like_matrix, lo, preferred_element_type=F32))


def _sigmoid(x):
    return 1.0 / (1.0 + jnp.exp(-x))


def _silu(x):
    return x * _sigmoid(x)


def _const_spec(shape):
    nd = len(shape)
    return pl.BlockSpec(shape, lambda *_: (0,) * nd)


def _params(*sem):
    return pltpu.CompilerParams(dimension_semantics=sem, vmem_limit_bytes=VMEM_LIMIT)


def _rope_kernel(pos_ref, f_ref, cos_ref, sin_ref):
    ang = pos_ref[0].astype(F32) * f_ref[...]
    cos_ref[0] = jnp.cos(ang)
    sin_ref[0] = jnp.sin(ang)


def _rope_tables(positions, freq128):
    B, S = positions.shape
    return pl.pallas_call(
        _rope_kernel,
        grid=(B,),
        in_specs=[pl.BlockSpec((1, S, 1), lambda b: (b, 0, 0)), _const_spec((1, LANE))],
        out_specs=[pl.BlockSpec((1, S, LANE), lambda b: (b, 0, 0))] * 2,
        out_shape=[jax.ShapeDtypeStruct((B, S, LANE), F32)] * 2,
        compiler_params=_params("arbitrary"),
        name="rope_tables",
    )(positions[..., None], freq128)


def _rwkv_kernel(x_ref, w_ref, mu_ref, w0_ref, w2_ref, a0_ref, a2_ref, kk_ref, ka_ref, rk_ref,
                 lng_ref, lnb_ref, seg_ref, o_ref,
                 carry_ref, z_ref, r_s, k_s, v_s, lw_s, a_s, b_s, g_s, y_s, *, ts, width):
    W = width
    P = 3 * W + 2 * RWKV_LORA
    n_pairs = W // LANE
    s_idx = pl.program_id(1)

    @pl.when(s_idx == 0)
    def _():
        carry_ref[...] = jnp.zeros_like(carry_ref)
        z_ref[...] = jnp.zeros_like(z_ref)

    h = jnp.dot(x_ref[0].astype(BF16), w_ref[...], preferred_element_type=F32)
    hp = h[:, :P]
    rows = lax.broadcasted_iota(jnp.int32, hp.shape, 0)
    prev = jnp.where(rows == 0, carry_ref[0:1, :], pltpu.roll(hp, 1, 0))
    carry_ref[0:1, :] = hp[ts - 1:ts, :]
    p = hp + mu_ref[...] * (prev - hp)
    r, k, v, wa = p[:, :W], p[:, W:2 * W], p[:, 2 * W:3 * W], p[:, 3 * W:]

    zz = -(w0_ref[...] + _dot(jnp.tanh(wa), w2_ref[...]))
    softplus = jnp.maximum(zz, 0.0) + jnp.log(1.0 + jnp.exp(-jnp.abs(zz)))
    lw_s[...] = -jnp.exp(-softplus - 0.5)
    a = _sigmoid(a0_ref[...] + _dot(wa, a2_ref[...]))
    kk = k * kk_ref[...]
    ss = _dot_0_1_rhs(kk * kk, seg_ref[...])
    kkn = kk * lax.rsqrt(jnp.maximum(ss, 1e-24))
    r_s[...] = r
    k_s[...] = k * (1.0 + (a - 1.0) * ka_ref[...])
    v_s[...] = v
    a_s[...] = -kkn
    b_s[...] = kkn * a
    g_s[...] = _silu(h[:, P:])

    C = CHUNK
    ri = lax.broadcasted_iota(jnp.int32, (2 * C, 2 * C), 0)
    ci = lax.broadcasted_iota(jnp.int32, (2 * C, 2 * C), 1)
    strict = ri > ci
    incl = ri >= ci
    eye = (ri == ci).astype(F32)
    tri = (lax.broadcasted_iota(jnp.int32, (C, C), 0) >= lax.broadcasted_iota(jnp.int32, (C, C), 1)).astype(BF16)
    first_head = lax.broadcasted_iota(jnp.int32, (C, LANE), 1) < RWKV_HEAD_DIM

    def stack(t):
        return jnp.concatenate([jnp.where(first_head, t, 0.0), jnp.where(first_head, 0.0, t)], axis=0)

    def chunk_body(c, carry):
        r0 = pl.multiple_of(c * C, C)
        rows_c = pl.ds(r0, C)
        lw = lw_s[rows_c, :]
        cs = _dot_0_1_lhs(tri, lw)
        ce = cs - lw
        tot = cs[C - 1:C, :]
        w_e, w_i, w_inv, w_end = jnp.exp(ce), jnp.exp(cs), jnp.exp(-cs), jnp.exp(tot - cs)
        e_tot = jnp.exp(tot)
        rr, kc, vc, ac, bc = r_s[rows_c, :], k_s[rows_c, :], v_s[rows_c, :], a_s[rows_c, :], b_s[rows_c, :]
        at, rt, bt, kt, bh, kh = ac * w_e, rr * w_i, bc * w_inv, kc * w_inv, bc * w_end, kc * w_end
        for pr in range(n_pairs):
            ls = slice(pr * LANE, (pr + 1) * LANE)
            As, Rs, Bs, Ks = stack(at[:, ls]), stack(rt[:, ls]), stack(bt[:, ls]), stack(kt[:, ls])
            Vs, Bh, Kh = stack(vc[:, ls]), stack(bh[:, ls]), stack(kh[:, ls])
            G = _dot_nt(jnp.concatenate([As, Rs], axis=0), jnp.concatenate([Bs, Ks], axis=0))
            L = jnp.where(strict, G[:2 * C, :2 * C], 0.0)
            Lak = jnp.where(strict, G[:2 * C, 2 * C:], 0.0)
            Mrb = jnp.where(incl, G[2 * C:, :2 * C], 0.0)
            Mrk = jnp.where(incl, G[2 * C:, 2 * C:], 0.0)
            T = eye + L
            Pw = _dot(L, L)
            step = 2
            while step * 2 < C:
                R2 = _dot(Pw, jnp.concatenate([Pw, T], axis=1))
                Pw, T = R2[:, :2 * C], T + R2[:, 2 * C:]
                step *= 2
            T = T + _dot(Pw, T)
            X = _dot(Lak, Vs)
            AU = _dot(T, jnp.concatenate([As, X], axis=1))
            Ah, U0 = AU[:, :LANE], AU[:, LANE:]
            Y0 = _dot(Mrk, Vs)
            Z = z_ref[pr]
            U = _dot_nt(Ah, Z) + U0
            Ys = _dot_nt(Rs, Z) + _dot(Mrb, U) + Y0
            y_s[rows_c, ls] = Ys[:C] + Ys[C:]
            z_ref[pr] = Z * e_tot[:, ls] + _dot_tn(jnp.concatenate([U, Vs], axis=0),
                                                   jnp.concatenate([Bh, Kh], axis=0))
        return carry

    lax.fori_loop(0, ts // C, chunk_body, 0)

    y = y_s[...]
    inv_n = 1.0 / RWKV_HEAD_DIM
    y_mu = _dot_0_1_rhs(y, seg_ref[...]) * inv_n
    d = y - y_mu
    y_var = _dot_0_1_rhs(d * d, seg_ref[...]) * inv_n
    yn = d * lax.rsqrt(y_var + RWKV_GN_EPS) * lng_ref[...] + lnb_ref[...]
    bonus = _dot_0_1_rhs(r_s[...] * k_s[...] * rk_ref[...], seg_ref[...]) * v_s[...]
    o_ref[0] = ((yn + bonus) * g_s[...]).astype(o_ref.dtype)


def _rwkv_branch(x, w_rw, mu, w0, w2p, a0, a2p, k_k, k_a, r_k, lnx_g, lnx_b, seg, ts):
    B, S, D = x.shape
    W = w0.shape[-1]
    P = 3 * W + 2 * RWKV_LORA
    n_pairs = W // LANE
    row = lambda n: _const_spec((1, n))
    scr = lambda: pltpu.VMEM((ts, W), F32)
    return pl.pallas_call(
        functools.partial(_rwkv_kernel, ts=ts, width=W),
        grid=(B, S // ts),
        in_specs=[pl.BlockSpec((1, ts, D), lambda b, s: (b, s, 0)),
                  _const_spec(w_rw.shape), row(P), row(W), _const_spec(w2p.shape), row(W),
                  _const_spec(a2p.shape), row(W), row(W), row(W), row(W), row(W), _const_spec(seg.shape)],
        out_specs=pl.BlockSpec((1, ts, W), lambda b, s: (b, s, 0)),
        out_shape=jax.ShapeDtypeStruct((B, S, W), BF16),
        scratch_shapes=[pltpu.VMEM((8, P), F32), pltpu.VMEM((n_pairs, LANE, LANE), F32),
                        scr(), scr(), scr(), scr(), scr(), scr(), scr(), scr()],
        compiler_params=_params("arbitrary", "arbitrary"),
        name="rwkv_branch",
    )(x, w_rw, mu, w0, w2p, a0, a2p, k_k, k_a, r_k, lnx_g, lnx_b, seg)


def _mla_prep_kernel(x_ref, w_ref, qn_ref, kvn_ref, wq_ref, wkv_ref, cos_ref, sin_ref,
                     q_ref, k_ref, v_ref, g_ref, *, q_lora, kv_lora, heads, scale):
    h = jnp.dot(x_ref[0].astype(BF16), w_ref[...], preferred_element_type=F32)
    ql = h[:, :q_lora]
    kvl = h[:, q_lora:q_lora + kv_lora]
    o = q_lora + kv_lora
    kpe, kpe_rot = h[:, o:o + LANE], h[:, o + LANE:o + 2 * LANE]
    gate = h[:, o + 2 * LANE:]
    cos, sin = cos_ref[0], sin_ref[0]
    qn = ql * lax.rsqrt(jnp.mean(ql * ql, axis=-1, keepdims=True) + RMS_EPS) * qn_ref[...]
    kvn = kvl * lax.rsqrt(jnp.mean(kvl * kvl, axis=-1, keepdims=True) + RMS_EPS) * kvn_ref[...]
    q2 = _dot(qn, wq_ref[...])
    kv = _dot(kvn, wkv_ref[...])
    hw = heads * LANE
    k_pe = kpe * cos + kpe_rot * sin
    for i in range(heads):
        ls = slice(i * LANE, (i + 1) * LANE)
        q_ref[0, :, ls] = ((q2[:, ls] * cos + q2[:, hw + i * LANE:hw + (i + 1) * LANE] * sin) * scale).astype(BF16)
        k_ref[0, :, ls] = (kv[:, ls] + k_pe).astype(BF16)
    v_ref[0] = kv[:, hw:].astype(BF16)
    g_ref[0] = _silu(gate).astype(BF16)


def _mla_prep(x, w_mla, q_norm, kv_norm, wq2, wkv2, cos, sin, heads, ts):
    B, S, D = x.shape
    q_lora, kv_lora = q_norm.shape[-1], kv_norm.shape[-1]
    W = heads * MLA_V
    tile = lambda n: pl.BlockSpec((1, ts, n), lambda b, s: (b, s, 0))
    return pl.pallas_call(
        functools.partial(_mla_prep_kernel, q_lora=q_lora, kv_lora=kv_lora, heads=heads,
                          scale=float(MLA_NOPE + MLA_ROPE) ** -0.5),
        grid=(B, S // ts),
        in_specs=[tile(D), _const_spec(w_mla.shape), _const_spec((1, q_lora)), _const_spec((1, kv_lora)),
                  _const_spec(wq2.shape), _const_spec(wkv2.shape), tile(LANE), tile(LANE)],
        out_specs=[tile(heads * LANE), tile(heads * LANE), tile(W), tile(W)],
        out_shape=[jax.ShapeDtypeStruct((B, S, heads * LANE), BF16), jax.ShapeDtypeStruct((B, S, heads * LANE), BF16),
                   jax.ShapeDtypeStruct((B, S, W), BF16), jax.ShapeDtypeStruct((B, S, W), BF16)],
        compiler_params=_params("arbitrary", "arbitrary"),
        name="mla_prep",
    )(x, w_mla, q_norm, kv_norm, wq2, wkv2, cos, sin)


def _mla_attn_kernel(q_ref, k_ref, v_ref, g_ref, o_ref, *, tq):
    i = pl.program_id(2)
    lane = lax.broadcasted_iota(jnp.int32, (tq, LANE), 1)
    causal = lax.broadcasted_iota(jnp.int32, (tq, tq), 0) >= lax.broadcasted_iota(jnp.int32, (tq, tq), 1)
    outs = []
    for e in range(2):
        ls = slice(e * LANE, (e + 1) * LANE)
        q = q_ref[0, :, ls]

        def tile_update(j, m, l, acc, masked):
            rows_k = pl.ds(pl.multiple_of(j * tq, tq), tq)
            s = lax.dot_general(q, k_ref[0, rows_k, ls], (((1,), (1,)), ((), ())), preferred_element_type=F32)
            if masked:
                s = jnp.where(causal, s, -1e30)
            m_new = jnp.maximum(m, jnp.max(s, axis=-1, keepdims=True))
            p = jnp.exp(s - m_new)
            alpha = jnp.exp(m - m_new)
            l = alpha * l + jnp.sum(p, axis=-1, keepdims=True)
            acc = alpha * acc + jnp.dot(p.astype(BF16), v_ref[0, rows_k, :], preferred_element_type=F32)
            return m_new, l, acc

        init = (jnp.full((tq, 1), -1e30, F32), jnp.zeros((tq, 1), F32), jnp.zeros((tq, LANE), F32))
        m, l, acc = lax.fori_loop(0, i, lambda j, c: tile_update(j, *c, masked=False), init)
        m, l, acc = tile_update(i, m, l, acc, masked=True)
        outs.append(acc / l)
    o = jnp.where(lane < MLA_V, outs[0], outs[1])
    o_ref[0] = (o * g_ref[0].astype(F32)).astype(o_ref.dtype)


def _mla_attention(q, k, v, g, tq):
    B, S, HW = q.shape
    W = v.shape[-1]
    n_pairs = W // LANE
    return pl.pallas_call(
        functools.partial(_mla_attn_kernel, tq=tq),
        grid=(B, n_pairs, S // tq),
        in_specs=[pl.BlockSpec((1, tq, 2 * LANE), lambda b, p, i: (b, i, p)),
                  pl.BlockSpec((1, S, 2 * LANE), lambda b, p, i: (b, 0, p)),
                  pl.BlockSpec((1, S, LANE), lambda b, p, i: (b, 0, p)),
                  pl.BlockSpec((1, tq, LANE), lambda b, p, i: (b, i, p))],
        out_specs=pl.BlockSpec((1, tq, LANE), lambda b, p, i: (b, i, p)),
        out_shape=jax.ShapeDtypeStruct((B, S, W), BF16),
        compiler_params=_params("arbitrary", "arbitrary", "arbitrary"),
        name="mla_attention",
    )(q, k, v, g)


def _conv_kernel(x_ref, w_ref, cw_ref, cb_ref, lg_ref, lb_ref, o_ref, hbuf, *, ts, width, row_block):
    W = width
    s_idx = pl.program_id(1)

    @pl.when(s_idx == 0)
    def _():
        hbuf[0:CONV_HALO, :] = jnp.zeros((CONV_HALO, W), F32)

    @pl.when(s_idx > 0)
    def _():
        hbuf[0:CONV_HALO, :] = hbuf[ts:ts + CONV_HALO, :]

    h = jnp.dot(x_ref[0].astype(BF16), w_ref[...], preferred_element_type=F32)
    hbuf[CONV_HALO:CONV_HALO + ts, :] = h[:, :W] * _sigmoid(h[:, W:2 * W])
    first_tap = CONV_HALO - (CONV_KERNEL - 1)
    for rb in range(ts // row_block):
        r0 = rb * row_block
        acc = jnp.zeros((row_block, W), F32) + cb_ref[...]
        for j in range(CONV_KERNEL):
            acc = acc + cw_ref[j:j + 1, :] * hbuf[r0 + first_tap + j:r0 + first_tap + j + row_block, :]
        mu = jnp.mean(acc, axis=-1, keepdims=True)
        d = acc - mu
        var = jnp.mean(d * d, axis=-1, keepdims=True)
        y = _silu(d * lax.rsqrt(var + LN_EPS) * lg_ref[...] + lb_ref[...])
        o_ref[0, r0:r0 + row_block, :] = (y * _silu(h[r0:r0 + row_block, 2 * W:])).astype(o_ref.dtype)


def _conv_branch(x, w_conv, conv_w, conv_b, ln_g, ln_b, ts):
    B, S, D = x.shape
    W = conv_b.shape[-1]
    return pl.pallas_call(
        functools.partial(_conv_kernel, ts=ts, width=W, row_block=32),
        grid=(B, S // ts),
        in_specs=[pl.BlockSpec((1, ts, D), lambda b, s: (b, s, 0)), _const_spec(w_conv.shape),
                  _const_spec(conv_w.shape), _const_spec((1, W)), _const_spec((1, W)), _const_spec((1, W))],
        out_specs=pl.BlockSpec((1, ts, W), lambda b, s: (b, s, 0)),
        out_shape=jax.ShapeDtypeStruct((B, S, W), BF16),
        scratch_shapes=[pltpu.VMEM((ts + CONV_HALO, W), F32)],
        compiler_params=_params("arbitrary", "arbitrary"),
        name="conv_branch",
    )(x, w_conv, conv_w, conv_b, ln_g, ln_b)


def _xattn_kernel(x_ref, mem_ref, w_ref, wkv_ref, o_ref, kv_s, *, width, scale):
    W = width

    @pl.when(pl.program_id(1) == 0)
    def _():
        kv_s[...] = jnp.dot(mem_ref[0].astype(BF16), wkv_ref[...], preferred_element_type=F32).astype(BF16)

    h = jnp.dot(x_ref[0].astype(BF16), w_ref[...], preferred_element_type=F32)
    for i in range(XATTN_HEADS):
        ls = slice(i * LANE, (i + 1) * LANE)
        s = _dot_nt(h[:, ls] * scale, kv_s[:, ls])
        p = jnp.exp(s - jnp.max(s, axis=-1, keepdims=True))
        o = _dot(p, kv_s[:, W + i * LANE:W + (i + 1) * LANE]) / jnp.sum(p, axis=-1, keepdims=True)
        o_ref[0, :, ls] = (o * _silu(h[:, W + i * LANE:W + (i + 1) * LANE])).astype(o_ref.dtype)


def _xattn_branch(x, mem, w_x, w_mem_kv, ts):
    B, S, D = x.shape
    M = mem.shape[1]
    W = w_x.shape[-1] // 2
    return pl.pallas_call(
        functools.partial(_xattn_kernel, width=W, scale=float(W // XATTN_HEADS) ** -0.5),
        grid=(B, S // ts),
        in_specs=[pl.BlockSpec((1, ts, D), lambda b, s: (b, s, 0)), pl.BlockSpec((1, M, D), lambda b, s: (b, 0, 0)),
                  _const_spec(w_x.shape), _const_spec(w_mem_kv.shape)],
        out_specs=pl.BlockSpec((1, ts, W), lambda b, s: (b, s, 0)),
        out_shape=jax.ShapeDtypeStruct((B, S, W), BF16),
        scratch_shapes=[pltpu.VMEM((M, 2 * W), BF16)],
        compiler_params=_params("arbitrary", "arbitrary"),
        name="xattn_branch",
    )(x, mem, w_x, w_mem_kv)


def _merge_kernel(x_ref, y0_ref, y1_ref, y2_ref, y3_ref, wm_ref, bg_ref, wo_ref, wout_ref, lg_ref, lb_ref,
                  o_ref, *, alpha):
    x = x_ref[0]
    xb = x.astype(BF16)
    D = x.shape[-1]
    merged = None
    for n, y_ref in enumerate((y0_ref, y1_ref, y2_ref, y3_ref)):
        gate = _sigmoid(jnp.dot(xb, wm_ref[:, n * D:(n + 1) * D], preferred_element_type=F32) + bg_ref[n:n + 1, :])
        term = gate * jnp.dot(y_ref[0], wo_ref[n], preferred_element_type=F32)
        merged = term if merged is None else merged + term
    z = alpha * x + _dot(merged, wout_ref[...])
    mu = jnp.mean(z, axis=-1, keepdims=True)
    d = z - mu
    var = jnp.mean(d * d, axis=-1, keepdims=True)
    o_ref[0] = d * lax.rsqrt(var + LN_EPS) * lg_ref[...] + lb_ref[...]


def _merge(x, ys, w_merge, b_gate, w_o, w_out, ln_g, ln_b, alpha, ts):
    B, S, D = x.shape
    W = ys[0].shape[-1]
    tile = lambda n: pl.BlockSpec((1, ts, n), lambda b, s: (b, s, 0))
    return pl.pallas_call(
        functools.partial(_merge_kernel, alpha=alpha),
        grid=(B, S // ts),
        in_specs=[tile(D), tile(W), tile(W), tile(W), tile(W), _const_spec(w_merge.shape),
                  _const_spec(b_gate.shape), _const_spec(w_o.shape), _const_spec(w_out.shape),
                  _const_spec((1, D)), _const_spec((1, D))],
        out_specs=tile(D),
        out_shape=jax.ShapeDtypeStruct((B, S, D), F32),
        compiler_params=_params("arbitrary", "arbitrary"),
        name="merge_out",
    )(x, *ys, w_merge, b_gate, w_o, w_out, ln_g, ln_b)


def _rotate_half_cols(w):
    half = w.shape[-1] // 2
    return jnp.concatenate([-w[..., half:], w[..., :half]], axis=-1)


def kernel(x, mem, positions, w_in, b_gate, rwkv_mu, rwkv_w0, rwkv_w2, rwkv_a0, rwkv_a2, rwkv_k_k, rwkv_k_a, rwkv_r_k, rwkv_lnx_g, rwkv_lnx_b, mla_q_norm, mla_w_uq, mla_kv_norm, mla_w_ukv, conv_w, conv_b, conv_ln_g, conv_ln_b, xattn_w_mem_kv, w_o_branch, w_out, ln_g, ln_b):
    B, S, D = x.shape
    depth = w_in.shape[0]
    W = D // 2
    q_lora, kv_lora = mla_q_norm.shape[-1], mla_kv_norm.shape[-1]
    heads = W // MLA_V
    P = 3 * W + 2 * RWKV_LORA
    alpha = (2.0 * depth) ** 0.25
    assert W % LANE == 0 and LANE == MLA_NOPE + MLA_V and MLA_NOPE + MLA_ROPE <= LANE
    ts = min(S, 256)
    assert S % ts == 0 and ts % CHUNK == 0

    sizes = (P, W, q_lora, kv_lora, MLA_ROPE, W, 2 * W, W, W, W, N_BRANCH * D)
    offs = [0]
    for n in sizes:
        offs.append(offs[-1] + n)
    o_rw, _, o_q, o_kv, o_kpe, o_mg, o_cu, _, o_xq, _, o_merge, o_end = offs
    assert o_end == w_in.shape[-1]

    inv_freq = ROPE_THETA ** (-jnp.arange(0, MLA_ROPE, 2, dtype=F32) / MLA_ROPE)
    zeros_f = lambda n: jnp.zeros((n,), F32)
    freq128 = jnp.concatenate([zeros_f(MLA_NOPE), inv_freq, inv_freq, zeros_f(LANE - MLA_NOPE - MLA_ROPE)])[None, :]
    cos, sin = _rope_tables(positions, freq128)

    lane_head = jnp.arange(W) // RWKV_HEAD_DIM
    seg = (lane_head[:, None] == lane_head[None, :]).astype(BF16)
    zl = jnp.zeros((RWKV_LORA, W), F32)
    row = lambda a: a.reshape(1, -1)

    for l in range(depth):
        wi = w_in[l]
        w_rw = wi[:, o_rw:o_q].astype(BF16)
        w2p = jnp.concatenate([rwkv_w2[l], zl], axis=0).astype(BF16)
        a2p = jnp.concatenate([zl, rwkv_a2[l]], axis=0).astype(BF16)
        y_rwkv = _rwkv_branch(x, w_rw, row(rwkv_mu[l]), row(rwkv_w0[l]), w2p, row(rwkv_a0[l]), a2p,
                              row(rwkv_k_k[l]), row(rwkv_k_a[l]), row(rwkv_r_k[l]), row(rwkv_lnx_g[l]),
                              row(rwkv_lnx_b[l]), seg, ts)

        kpe_w = wi[:, o_kpe:o_mg]
        pad_l, pad_r = jnp.zeros((D, MLA_NOPE), F32), jnp.zeros((D, LANE - MLA_NOPE - MLA_ROPE), F32)
        w_mla = jnp.concatenate([wi[:, o_q:o_kpe], pad_l, kpe_w, pad_r, pad_l, _rotate_half_cols(kpe_w), pad_r,
                                 wi[:, o_mg:o_cu]], axis=1).astype(BF16)
        wq = mla_w_uq[l].reshape(q_lora, heads, MLA_NOPE + MLA_ROPE)
        q_nope, q_pe = wq[..., :MLA_NOPE], wq[..., MLA_NOPE:]
        zq = lambda n: jnp.zeros((q_lora, heads, n), F32)
        wq_plain = jnp.concatenate([q_nope, q_pe, zq(LANE - MLA_NOPE - MLA_ROPE)], axis=-1)
        wq_rot = jnp.concatenate([zq(MLA_NOPE), _rotate_half_cols(q_pe), zq(LANE - MLA_NOPE - MLA_ROPE)], axis=-1)
        wq2 = jnp.concatenate([wq_plain.reshape(q_lora, -1), wq_rot.reshape(q_lora, -1)], axis=1).astype(BF16)
        wkv = mla_w_ukv[l].reshape(kv_lora, heads, MLA_NOPE + MLA_V)
        wk = jnp.concatenate([wkv[..., :MLA_NOPE], jnp.zeros((kv_lora, heads, LANE - MLA_NOPE), F32)], axis=-1)
        wkv2 = jnp.concatenate([wk.reshape(kv_lora, -1), wkv[..., MLA_NOPE:].reshape(kv_lora, -1)],
                               axis=1).astype(BF16)
        q, k, v, g = _mla_prep(x, w_mla, row(mla_q_norm[l]), row(mla_kv_norm[l]), wq2, wkv2, cos, sin, heads, ts)
        y_mla = _mla_attention(q, k, v, g, ts)

        y_conv = _conv_branch(x, wi[:, o_cu:o_xq].astype(BF16), conv_w[l], row(conv_b[l]),
                              row(conv_ln_g[l]), row(conv_ln_b[l]), ts)
        y_mem = _xattn_branch(x, mem, wi[:, o_xq:o_merge].astype(BF16), xattn_w_mem_kv[l].astype(BF16), ts)

        x = _merge(x, (y_rwkv, y_mla, y_conv, y_mem), wi[:, o_merge:].astype(BF16), b_gate[l],
                   w_o_branch[l].astype(BF16), w_out[l].astype(BF16), row(ln_g[l]), row(ln_b[l]), alpha, ts)
    return x
```

```python
import functools

import jax
import jax.numpy as jnp
from jax import lax
from jax.experimental import pallas as pl
from jax.experimental.pallas import tpu as pltpu

F32 = jnp.float32
BF16 = jnp.bfloat16

N_BRANCH = 4
RWKV_HEAD_DIM = 64
RWKV_LORA = 64
RWKV_GN_EPS = 64e-5
MLA_NOPE = 64
MLA_ROPE = 32
MLA_V = 64
ROPE_THETA = 10000.0
CONV_KERNEL = 31
XATTN_HEADS = 4
LN_EPS = 1e-5
RMS_EPS = 1e-6

LANE = 128
SUBLANE = 8
CHUNK = 64
CONV_HALO = 32
VMEM_LIMIT = 56 * 1024 * 1024


def _dot(a, b):
    return jnp.dot(a.astype(BF16), b.astype(BF16), preferred_element_type=F32)


def _dot_nt(a, b):
    return lax.dot_general(a.astype(BF16), b.astype(BF16), (((1,), (1,)), ((), ())),
                           preferred_element_type=F32)


def _split_hi_lo(a):
    hi = a.astype(BF16)
    lo = (a - hi.astype(F32)).astype(BF16)
    return hi, lo


def _dot_0_1_rhs(a, ones_like_matrix):
    hi, lo = _split_hi_lo(a)
    return (jnp.dot(hi, ones_like_matrix, preferred_element_type=F32)
            + jnp.dot(lo, ones_like_matrix, preferred_element_type=F32))


def _sigmoid(x):
    return 1.0 / (1.0 + jnp.exp(-x))


def _silu(x):
    return x * _sigmoid(x)


def _const_spec(shape):
    nd = len(shape)
    return pl.BlockSpec(shape, lambda *_: (0,) * nd)


def _params(*sem):
    return pltpu.CompilerParams(dimension_semantics=sem, vmem_limit_bytes=VMEM_LIMIT)


def _rope_kernel(pos_ref, f_ref, cos_ref, sin_ref):
    ang = pos_ref[0].astype(F32) * f_ref[...]
    cos_ref[0] = jnp.cos(ang)
    sin_ref[0] = jnp.sin(ang)


def _rope_tables(positions, freq128):
    B, S = positions.shape
    return pl.pallas_call(
        _rope_kernel,
        grid=(B,),
        in_specs=[pl.BlockSpec((1, S, 1), lambda b: (b, 0, 0)), _const_spec((1, LANE))],
        out_specs=[pl.BlockSpec((1, S, LANE), lambda b: (b, 0, 0))] * 2,
        out_shape=[jax.ShapeDtypeStruct((B, S, LANE), F32)] * 2,
        compiler_params=_params("arbitrary"),
        name="rope_tables",
    )(positions[..., None], freq128)


def _rwkv_kernel(x_ref, w_ref, mu_ref, w0_ref, w2_ref, a0_ref, a2_ref, kk_ref, ka_ref, rk_ref,
                 lng_ref, lnb_ref, seg_ref, o_ref,
                 carry_ref, z_ref, r_s, k_s, v_s, g_s, at_s, rt_s, bt_s, kt_s, bh_s, kh_s, et_s, y_s, *, ts, width):
    W = width
    P = 3 * W + 2 * RWKV_LORA
    n_pairs = W // LANE
    s_idx = pl.program_id(1)

    @pl.when(s_idx == 0)
    def _():
        carry_ref[...] = jnp.zeros_like(carry_ref)
        z_ref[...] = jnp.zeros_like(z_ref)

    h = jnp.dot(x_ref[0].astype(BF16), w_ref[...], preferred_element_type=F32)
    hp = h[:, :P]
    rows = lax.broadcasted_iota(jnp.int32, hp.shape, 0)
    prev = jnp.where(rows == 0, carry_ref[0:1, :], pltpu.roll(hp, 1, 0))
    carry_ref[0:1, :] = hp[ts - 1:ts, :]
    p = hp + mu_ref[...] * (prev - hp)
    r, k, v, wa = p[:, :W], p[:, W:2 * W], p[:, 2 * W:3 * W], p[:, 3 * W:]

    zz = -(w0_ref[...] + _dot(jnp.tanh(wa), w2_ref[...]))
    softplus = jnp.maximum(zz, 0.0) + jnp.log(1.0 + jnp.exp(-jnp.abs(zz)))
    lw = -jnp.exp(-softplus - 0.5)
    a = _sigmoid(a0_ref[...] + _dot(wa, a2_ref[...]))
    kk = k * kk_ref[...]
    ss = _dot_0_1_rhs(kk * kk, seg_ref[...])
    kkn = kk * lax.rsqrt(jnp.maximum(ss, 1e-24))
    kmod = k * (1.0 + (a - 1.0) * ka_ref[...])
    b = kkn * a

    C = CHUNK
    n_chunks = ts // C
    rr = lax.broadcasted_iota(jnp.int32, (ts, ts), 0)
    cc = lax.broadcasted_iota(jnp.int32, (ts, ts), 1)
    same_chunk = (rr // C) == (cc // C)
    lw_hi, lw_lo = _split_hi_lo(lw)
    tri = jnp.where(same_chunk & (rr >= cc), 1.0, 0.0).astype(BF16)
    blk = jnp.where(same_chunk, 1.0, 0.0).astype(BF16)
    cs = jnp.dot(tri, lw_hi, preferred_element_type=F32) + jnp.dot(tri, lw_lo, preferred_element_type=F32)
    tot = jnp.dot(blk, lw_hi, preferred_element_type=F32) + jnp.dot(blk, lw_lo, preferred_element_type=F32)
    w_inv, w_end = jnp.exp(-cs), jnp.exp(tot - cs)
    r_s[...] = r
    k_s[...] = kmod
    v_s[...] = v
    g_s[...] = _silu(h[:, P:])
    at_s[...] = -kkn * jnp.exp(cs - lw)
    rt_s[...] = r * jnp.exp(cs)
    bt_s[...] = b * w_inv
    kt_s[...] = kmod * w_inv
    bh_s[...] = b * w_end
    kh_s[...] = kmod * w_end
    et_s[...] = jnp.exp(tot)

    units = [(c, pr) for c in range(n_chunks) for pr in range(n_pairs)]
    first_head = lax.broadcasted_iota(jnp.int32, (C, LANE), 1) < RWKV_HEAD_DIM

    def stacked(ref):
        out = []
        for c, pr in units:
            t = ref[c * C:(c + 1) * C, pr * LANE:(pr + 1) * LANE]
            out.append(jnp.concatenate([jnp.where(first_head, t, 0.0), jnp.where(first_head, 0.0, t)],
                                       axis=0).astype(BF16))
        return jnp.stack(out)

    def bmm(x, y):
        return jnp.einsum('umk,ukn->umn', x.astype(BF16), y.astype(BF16), preferred_element_type=F32)

    def bmm_nt(x, y):
        return jnp.einsum('umk,unk->umn', x.astype(BF16), y.astype(BF16), preferred_element_type=F32)

    def bmm_tn(x, y):
        return jnp.einsum('ukm,ukn->umn', x.astype(BF16), y.astype(BF16), preferred_element_type=F32)

    ri = lax.broadcasted_iota(jnp.int32, (2 * C, 2 * C), 0)
    ci = lax.broadcasted_iota(jnp.int32, (2 * C, 2 * C), 1)
    strict = (ri > ci)[None]
    incl = (ri >= ci)[None]
    eye = (ri == ci).astype(F32)[None]

    As, Rs, Bs, Ks = stacked(at_s), stacked(rt_s), stacked(bt_s), stacked(kt_s)
    Vs, Bh, Kh = stacked(v_s), stacked(bh_s), stacked(kh_s)
    G = bmm_nt(jnp.concatenate([As, Rs], axis=1), jnp.concatenate([Bs, Ks], axis=1))
    L = jnp.where(strict, G[:, :2 * C, :2 * C], 0.0)
    Lak = jnp.where(strict, G[:, :2 * C, 2 * C:], 0.0)
    Mrb = jnp.where(incl, G[:, 2 * C:, :2 * C], 0.0)
    Mrk = jnp.where(incl, G[:, 2 * C:, 2 * C:], 0.0)
    T = eye + L
    Lb = L.astype(BF16)
    Pw = bmm(Lb, Lb)
    step = 2
    while step * 2 < C:
        Pb = Pw.astype(BF16)
        R2 = bmm(Pb, jnp.concatenate([Pb, T.astype(BF16)], axis=2))
        Pw, T = R2[:, :, :2 * C], T + R2[:, :, 2 * C:]
        step *= 2
    T = T + bmm(Pw, T)
    X = bmm(Lak, Vs)
    AU = bmm(T, jnp.concatenate([As, X.astype(BF16)], axis=2))
    Ah, U0 = AU[:, :, :LANE], AU[:, :, LANE:]
    Y0 = bmm(Mrk, Vs)
    GH = bmm_tn(AU, Bh)
    Gc = GH[:, :LANE]
    Hc = GH[:, LANE:] + bmm_tn(Vs, Kh)
    Z = z_ref[...]
    starts = []
    for c in range(n_chunks):
        us = slice(c * n_pairs, (c + 1) * n_pairs)
        e_tot = jnp.stack([et_s[c * C:c * C + 1, pr * LANE:(pr + 1) * LANE] for pr in range(n_pairs)])
        starts.append(Z)
        Z = Z * e_tot + bmm(Z, Gc[us]) + Hc[us]
    z_ref[...] = Z
    Z0 = jnp.concatenate(starts, axis=0)
    AR = bmm_nt(jnp.concatenate([Ah.astype(BF16), Rs], axis=1), Z0)
    Uu = AR[:, :2 * C] + U0
    Ys = AR[:, 2 * C:] + bmm(Mrb, Uu) + Y0
    ysum = Ys[:, :C] + Ys[:, C:]
    for u, (c, pr) in enumerate(units):
        y_s[c * C:(c + 1) * C, pr * LANE:(pr + 1) * LANE] = ysum[u]

    y = y_s[...]
    inv_n = 1.0 / RWKV_HEAD_DIM
    y_mu = _dot_0_1_rhs(y, seg_ref[...]) * inv_n
    d = y - y_mu
    y_var = _dot_0_1_rhs(d * d, seg_ref[...]) * inv_n
    yn = d * lax.rsqrt(y_var + RWKV_GN_EPS) * lng_ref[...] + lnb_ref[...]
    bonus = _dot_0_1_rhs(r_s[...] * k_s[...] * rk_ref[...], seg_ref[...]) * v_s[...]
    o_ref[0] = ((yn + bonus) * g_s[...]).astype(o_ref.dtype)


def _rwkv_branch(x, w_rw, mu, w0, w2p, a0, a2p, k_k, k_a, r_k, lnx_g, lnx_b, seg, ts):
    B, S, D = x.shape
    W = w0.shape[-1]
    P = 3 * W + 2 * RWKV_LORA
    n_pairs = W // LANE
    row = lambda n: _const_spec((1, n))
    scr = lambda: pltpu.VMEM((ts, W), F32)
    return pl.pallas_call(
        functools.partial(_rwkv_kernel, ts=ts, width=W),
        grid=(B, S // ts),
        in_specs=[pl.BlockSpec((1, ts, D), lambda b, s: (b, s, 0)),
                  _const_spec(w_rw.shape), row(P), row(W), _const_spec(w2p.shape), row(W),
                  _const_spec(a2p.shape), row(W), row(W), row(W), row(W), row(W), _const_spec(seg.shape)],
        out_specs=pl.BlockSpec((1, ts, W), lambda b, s: (b, s, 0)),
        out_shape=jax.ShapeDtypeStruct((B, S, W), BF16),
        scratch_shapes=[pltpu.VMEM((8, P), F32), pltpu.VMEM((n_pairs, LANE, LANE), F32),
                        scr(), scr(), scr(), scr(), scr(), scr(), scr(), scr(), scr(), scr(), scr(), scr()],
        compiler_params=_params("arbitrary", "arbitrary"),
        name="rwkv_branch",
    )(x, w_rw, mu, w0, w2p, a0, a2p, k_k, k_a, r_k, lnx_g, lnx_b, seg)


def _mla_prep_kernel(x_ref, w_ref, qn_ref, kvn_ref, wq_ref, wkv_ref, cos_ref, sin_ref,
                     q_ref, k_ref, v_ref, g_ref, *, q_lora, kv_lora, heads, scale):
    h = jnp.dot(x_ref[0].astype(BF16), w_ref[...], preferred_element_type=F32)
    ql = h[:, :q_lora]
    kvl = h[:, q_lora:q_lora + kv_lora]
    o = q_lora + kv_lora
    kpe, kpe_rot = h[:, o:o + LANE], h[:, o + LANE:o + 2 * LANE]
    gate = h[:, o + 2 * LANE:]
    cos, sin = cos_ref[0], sin_ref[0]
    qn = ql * lax.rsqrt(jnp.mean(ql * ql, axis=-1, keepdims=True) + RMS_EPS) * qn_ref[...]
    kvn = kvl * lax.rsqrt(jnp.mean(kvl * kvl, axis=-1, keepdims=True) + RMS_EPS) * kvn_ref[...]
    q2 = _dot(qn, wq_ref[...])
    kv = _dot(kvn, wkv_ref[...])
    hw = heads * LANE
    k_pe = kpe * cos + kpe_rot * sin
    ones_lane = jnp.where(lax.broadcasted_iota(jnp.int32, (1, LANE), 1) == MLA_V, 1.0, 0.0)
    for i in range(heads):
        ls = slice(i * LANE, (i + 1) * LANE)
        q_ref[0, :, ls] = ((q2[:, ls] * cos + q2[:, hw + i * LANE:hw + (i + 1) * LANE] * sin) * scale).astype(BF16)
        k_ref[0, :, ls] = (kv[:, ls] + k_pe).astype(BF16)
        v_ref[0, :, ls] = (kv[:, hw + i * LANE:hw + (i + 1) * LANE] + ones_lane).astype(BF16)
    g_ref[0] = _silu(gate).astype(BF16)


def _mla_prep(x, w_mla, q_norm, kv_norm, wq2, wkv2, cos, sin, heads, ts):
    B, S, D = x.shape
    q_lora, kv_lora = q_norm.shape[-1], kv_norm.shape[-1]
    W = heads * MLA_V
    tile = lambda n: pl.BlockSpec((1, ts, n), lambda b, s: (b, s, 0))
    return pl.pallas_call(
        functools.partial(_mla_prep_kernel, q_lora=q_lora, kv_lora=kv_lora, heads=heads,
                          scale=float(MLA_NOPE + MLA_ROPE) ** -0.5),
        grid=(B, S // ts),
        in_specs=[tile(D), _const_spec(w_mla.shape), _const_spec((1, q_lora)), _const_spec((1, kv_lora)),
                  _const_spec(wq2.shape), _const_spec(wkv2.shape), tile(LANE), tile(LANE)],
        out_specs=[tile(heads * LANE), tile(heads * LANE), tile(heads * LANE), tile(W)],
        out_shape=[jax.ShapeDtypeStruct((B, S, heads * LANE), BF16)] * 3 + [jax.ShapeDtypeStruct((B, S, W), BF16)],
        compiler_params=_params("arbitrary", "arbitrary"),
        name="mla_prep",
    )(x, w_mla, q_norm, kv_norm, wq2, wkv2, cos, sin)


def _mla_attn_kernel(q_ref, k_ref, v_ref, g_ref, o_ref, *, tq, tk):
    i = pl.program_id(2)
    n_blk = tq // tk
    streams = [(e, hb) for e in range(2) for hb in range(n_blk)]
    causal = lax.broadcasted_iota(jnp.int32, (tk, tk), 0) >= lax.broadcasted_iota(jnp.int32, (tk, tk), 1)

    def visit(j, state, mode):
        rows_k = pl.ds(pl.multiple_of(j * tk, tk), tk)
        new = []
        for (e, hb), (m, acc) in zip(streams, state):
            if mode[hb] is None:
                new.append((m, acc))
                continue
            ls = slice(e * LANE, (e + 1) * LANE)
            s = lax.dot_general(q_ref[0, hb * tk:(hb + 1) * tk, ls], k_ref[0, rows_k, ls],
                                (((1,), (1,)), ((), ())), preferred_element_type=F32)
            if mode[hb]:
                s = jnp.where(causal, s, -1e30)
            m_new = jnp.maximum(m, jnp.max(s, axis=-1, keepdims=True))
            p = jnp.exp(s - m_new)
            acc = jnp.exp(m - m_new) * acc + jnp.dot(p.astype(BF16), v_ref[0, rows_k, ls],
                                                     preferred_element_type=F32)
            new.append((m_new, acc))
        return tuple(new)

    state = tuple((jnp.full((tk, 1), -1e30, F32), jnp.zeros((tk, LANE), F32)) for _ in streams)
    state = lax.fori_loop(0, n_blk * i, lambda j, st: visit(j, st, [False] * n_blk), state)
    for d in range(n_blk):
        state = visit(n_blk * i + d, state, [None if hb < d else hb == d for hb in range(n_blk)])

    lane = lax.broadcasted_iota(jnp.int32, (tk, LANE), 1)
    acc = dict(zip(streams, (st[1] for st in state)))
    for hb in range(n_blk):
        rows = slice(hb * tk, (hb + 1) * tk)
        a0, a1 = acc[(0, hb)], acc[(1, hb)]
        o = jnp.where(lane < MLA_V, a0 / a0[:, MLA_V:MLA_V + 1],
                      pltpu.roll(a1, MLA_V, 1) / a1[:, MLA_V:MLA_V + 1])
        o_ref[0, rows, :] = (o * g_ref[0, rows, :].astype(F32)).astype(o_ref.dtype)


def _mla_attention(q, k, v, g, tq, tk):
    B, S, HW = q.shape
    W = g.shape[-1]
    n_pairs = W // LANE
    return pl.pallas_call(
        functools.partial(_mla_attn_kernel, tq=tq, tk=tk),
        grid=(B, n_pairs, S // tq),
        in_specs=[pl.BlockSpec((1, tq, 2 * LANE), lambda b, p, i: (b, i, p)),
                  pl.BlockSpec((1, S, 2 * LANE), lambda b, p, i: (b, 0, p)),
                  pl.BlockSpec((1, S, 2 * LANE), lambda b, p, i: (b, 0, p)),
                  pl.BlockSpec((1, tq, LANE), lambda b, p, i: (b, i, p))],
        out_specs=pl.BlockSpec((1, tq, LANE), lambda b, p, i: (b, i, p)),
        out_shape=jax.ShapeDtypeStruct((B, S, W), BF16),
        compiler_params=_params("arbitrary", "arbitrary", "arbitrary"),
        name="mla_attention",
    )(q, k, v, g)


def _conv_kernel(x_ref, w_ref, cw_ref, cb_ref, lg_ref, lb_ref, o_ref, hbuf, *, ts, width, row_block):
    W = width
    s_idx = pl.program_id(1)

    @pl.when(s_idx == 0)
    def _():
        hbuf[0, 0:CONV_HALO, :] = jnp.zeros((CONV_HALO, W), F32)

    @pl.when(s_idx > 0)
    def _():
        hbuf[0, 0:CONV_HALO, :] = hbuf[0, ts:ts + CONV_HALO, :]

    h = jnp.dot(x_ref[0].astype(BF16), w_ref[...], preferred_element_type=F32)
    hbuf[0, CONV_HALO:CONV_HALO + ts, :] = h[:, :W] * _sigmoid(h[:, W:2 * W])
    shifted_rows = ts + CONV_HALO - SUBLANE
    for r in range(1, SUBLANE):
        hbuf[r, 0:shifted_rows, :] = hbuf[0, r:r + shifted_rows, :]
    first_tap = CONV_HALO - (CONV_KERNEL - 1)
    for rb in range(ts // row_block):
        r0 = rb * row_block
        acc = jnp.zeros((row_block, W), F32) + cb_ref[...]
        for j in range(CONV_KERNEL):
            q8, r = divmod(first_tap + j, SUBLANE)
            acc = acc + cw_ref[j:j + 1, :] * hbuf[r, r0 + q8 * SUBLANE:r0 + q8 * SUBLANE + row_block, :]
        mu = jnp.mean(acc, axis=-1, keepdims=True)
        d = acc - mu
        var = jnp.mean(d * d, axis=-1, keepdims=True)
        y = _silu(d * lax.rsqrt(var + LN_EPS) * lg_ref[...] + lb_ref[...])
        o_ref[0, r0:r0 + row_block, :] = (y * _silu(h[r0:r0 + row_block, 2 * W:])).astype(o_ref.dtype)


def _conv_branch(x, w_conv, conv_w, conv_b, ln_g, ln_b, ts):
    B, S, D = x.shape
    W = conv_b.shape[-1]
    return pl.pallas_call(
        functools.partial(_conv_kernel, ts=ts, width=W, row_block=32),
        grid=(B, S // ts),
        in_specs=[pl.BlockSpec((1, ts, D), lambda b, s: (b, s, 0)), _const_spec(w_conv.shape),
                  _const_spec(conv_w.shape), _const_spec((1, W)), _const_spec((1, W)), _const_spec((1, W))],
        out_specs=pl.BlockSpec((1, ts, W), lambda b, s: (b, s, 0)),
        out_shape=jax.ShapeDtypeStruct((B, S, W), BF16),
        scratch_shapes=[pltpu.VMEM((SUBLANE, ts + CONV_HALO, W), F32)],
        compiler_params=_params("arbitrary", "arbitrary"),
        name="conv_branch",
    )(x, w_conv, conv_w, conv_b, ln_g, ln_b)


def _xattn_kernel(x_ref, mem_ref, w_ref, wkv_ref, o_ref, kv_s, *, width, scale):
    W = width

    @pl.when(pl.program_id(1) == 0)
    def _():
        kv_s[...] = jnp.dot(mem_ref[0].astype(BF16), wkv_ref[...], preferred_element_type=F32).astype(BF16)

    h = jnp.dot(x_ref[0].astype(BF16), w_ref[...], preferred_element_type=F32)
    for i in range(XATTN_HEADS):
        ls = slice(i * LANE, (i + 1) * LANE)
        s = _dot_nt(h[:, ls] * scale, kv_s[:, ls])
        p = jnp.exp(s - jnp.max(s, axis=-1, keepdims=True))
        o = _dot(p, kv_s[:, W + i * LANE:W + (i + 1) * LANE]) / jnp.sum(p, axis=-1, keepdims=True)
        o_ref[0, :, ls] = (o * _silu(h[:, W + i * LANE:W + (i + 1) * LANE])).astype(o_ref.dtype)


def _xattn_branch(x, mem, w_x, w_mem_kv, ts):
    B, S, D = x.shape
    M = mem.shape[1]
    W = w_x.shape[-1] // 2
    return pl.pallas_call(
        functools.partial(_xattn_kernel, width=W, scale=float(W // XATTN_HEADS) ** -0.5),
        grid=(B, S // ts),
        in_specs=[pl.BlockSpec((1, ts, D), lambda b, s: (b, s, 0)), pl.BlockSpec((1, M, D), lambda b, s: (b, 0, 0)),
                  _const_spec(w_x.shape), _const_spec(w_mem_kv.shape)],
        out_specs=pl.BlockSpec((1, ts, W), lambda b, s: (b, s, 0)),
        out_shape=jax.ShapeDtypeStruct((B, S, W), BF16),
        scratch_shapes=[pltpu.VMEM((M, 2 * W), BF16)],
        compiler_params=_params("arbitrary", "arbitrary"),
        name="xattn_branch",
    )(x, mem, w_x, w_mem_kv)


def _merge_kernel(x_ref, y0_ref, y1_ref, y2_ref, y3_ref, wm_ref, bg_ref, wo_ref, wout_ref, lg_ref, lb_ref,
                  o_ref, *, alpha):
    x = x_ref[0]
    xb = x.astype(BF16)
    D = x.shape[-1]
    merged = None
    for n, y_ref in enumerate((y0_ref, y1_ref, y2_ref, y3_ref)):
        gate = _sigmoid(jnp.dot(xb, wm_ref[:, n * D:(n + 1) * D], preferred_element_type=F32) + bg_ref[n:n + 1, :])
        term = gate * jnp.dot(y_ref[0], wo_ref[n], preferred_element_type=F32)
        merged = term if merged is None else merged + term
    z = alpha * x + _dot(merged, wout_ref[...])
    mu = jnp.mean(z, axis=-1, keepdims=True)
    d = z - mu
    var = jnp.mean(d * d, axis=-1, keepdims=True)
    o_ref[0] = d * lax.rsqrt(var + LN_EPS) * lg_ref[...] + lb_ref[...]


def _merge(x, ys, w_merge, b_gate, w_o, w_out, ln_g, ln_b, alpha, ts):
    B, S, D = x.shape
    W = ys[0].shape[-1]
    tile = lambda n: pl.BlockSpec((1, ts, n), lambda b, s: (b, s, 0))
    return pl.pallas_call(
        functools.partial(_merge_kernel, alpha=alpha),
        grid=(B, S // ts),
        in_specs=[tile(D), tile(W), tile(W), tile(W), tile(W), _const_spec(w_merge.shape),
                  _const_spec(b_gate.shape), _const_spec(w_o.shape), _const_spec(w_out.shape),
                  _const_spec((1, D)), _const_spec((1, D))],
        out_specs=tile(D),
        out_shape=jax.ShapeDtypeStruct((B, S, D), F32),
        compiler_params=_params("arbitrary", "arbitrary"),
        name="merge_out",
    )(x, *ys, w_merge, b_gate, w_o, w_out, ln_g, ln_b)


def _rotate_half_cols(w):
    half = w.shape[-1] // 2
    return jnp.concatenate([-w[..., half:], w[..., :half]], axis=-1)


def kernel(x, mem, positions, w_in, b_gate, rwkv_mu, rwkv_w0, rwkv_w2, rwkv_a0, rwkv_a2, rwkv_k_k, rwkv_k_a, rwkv_r_k, rwkv_lnx_g, rwkv_lnx_b, mla_q_norm, mla_w_uq, mla_kv_norm, mla_w_ukv, conv_w, conv_b, conv_ln_g, conv_ln_b, xattn_w_mem_kv, w_o_branch, w_out, ln_g, ln_b):
    B, S, D = x.shape
    depth = w_in.shape[0]
    W = D // 2
    q_lora, kv_lora = mla_q_norm.shape[-1], mla_kv_norm.shape[-1]
    heads = W // MLA_V
    P = 3 * W + 2 * RWKV_LORA
    alpha = (2.0 * depth) ** 0.25
    assert W % LANE == 0 and LANE == MLA_NOPE + MLA_V and MLA_NOPE + MLA_ROPE <= LANE
    ts = min(S, 256)
    tk = min(S, 256)
    tq = min(S, 2 * tk)
    assert S % ts == 0 and ts % CHUNK == 0 and S % tq == 0 and tq % tk == 0

    sizes = (P, W, q_lora, kv_lora, MLA_ROPE, W, 2 * W, W, W, W, N_BRANCH * D)
    offs = [0]
    for n in sizes:
        offs.append(offs[-1] + n)
    o_rw, _, o_q, o_kv, o_kpe, o_mg, o_cu, _, o_xq, _, o_merge, o_end = offs
    assert o_end == w_in.shape[-1]

    inv_freq = ROPE_THETA ** (-jnp.arange(0, MLA_ROPE, 2, dtype=F32) / MLA_ROPE)
    zeros_f = lambda n: jnp.zeros((n,), F32)
    freq128 = jnp.concatenate([zeros_f(MLA_NOPE), inv_freq, inv_freq, zeros_f(LANE - MLA_NOPE - MLA_ROPE)])[None, :]
    cos, sin = _rope_tables(positions, freq128)

    lane_head = jnp.arange(W) // RWKV_HEAD_DIM
    seg = (lane_head[:, None] == lane_head[None, :]).astype(BF16)
    zl = jnp.zeros((RWKV_LORA, W), F32)
    row = lambda a: a.reshape(1, -1)

    for l in range(depth):
        wi = w_in[l]
        w_rw = wi[:, o_rw:o_q].astype(BF16)
        w2p = jnp.concatenate([rwkv_w2[l], zl], axis=0).astype(BF16)
        a2p = jnp.concatenate([zl, rwkv_a2[l]], axis=0).astype(BF16)
        y_rwkv = _rwkv_branch(x, w_rw, row(rwkv_mu[l]), row(rwkv_w0[l]), w2p, row(rwkv_a0[l]), a2p,
                              row(rwkv_k_k[l]), row(rwkv_k_a[l]), row(rwkv_r_k[l]), row(rwkv_lnx_g[l]),
                              row(rwkv_lnx_b[l]), seg, ts)

        kpe_w = wi[:, o_kpe:o_mg]
        pad_l, pad_r = jnp.zeros((D, MLA_NOPE), F32), jnp.zeros((D, LANE - MLA_NOPE - MLA_ROPE), F32)
        w_mla = jnp.concatenate([wi[:, o_q:o_kpe], pad_l, kpe_w, pad_r, pad_l, _rotate_half_cols(kpe_w), pad_r,
                                 wi[:, o_mg:o_cu]], axis=1).astype(BF16)
        wq = mla_w_uq[l].reshape(q_lora, heads, MLA_NOPE + MLA_ROPE)
        q_nope, q_pe = wq[..., :MLA_NOPE], wq[..., MLA_NOPE:]
        zq = lambda n: jnp.zeros((q_lora, heads, n), F32)
        wq_plain = jnp.concatenate([q_nope, q_pe, zq(LANE - MLA_NOPE - MLA_ROPE)], axis=-1)
        wq_rot = jnp.concatenate([zq(MLA_NOPE), _rotate_half_cols(q_pe), zq(LANE - MLA_NOPE - MLA_ROPE)], axis=-1)
        wq2 = jnp.concatenate([wq_plain.reshape(q_lora, -1), wq_rot.reshape(q_lora, -1)], axis=1).astype(BF16)
        wkv = mla_w_ukv[l].reshape(kv_lora, heads, MLA_NOPE + MLA_V)
        wk = jnp.concatenate([wkv[..., :MLA_NOPE], jnp.zeros((kv_lora, heads, LANE - MLA_NOPE), F32)], axis=-1)
        wv = jnp.concatenate([wkv[..., MLA_NOPE:], jnp.zeros((kv_lora, heads, LANE - MLA_V), F32)], axis=-1)
        wkv2 = jnp.concatenate([wk.reshape(kv_lora, -1), wv.reshape(kv_lora, -1)], axis=1).astype(BF16)
        q, k, v, g = _mla_prep(x, w_mla, row(mla_q_norm[l]), row(mla_kv_norm[l]), wq2, wkv2, cos, sin, heads, ts)
        y_mla = _mla_attention(q, k, v, g, tq, tk)

        y_conv = _conv_branch(x, wi[:, o_cu:o_xq].astype(BF16), conv_w[l], row(conv_b[l]),
                              row(conv_ln_g[l]), row(conv_ln_b[l]), ts)
        y_mem = _xattn_branch(x, mem, wi[:, o_xq:o_merge].astype(BF16), xattn_w_mem_kv[l].astype(BF16), ts)

        x = _merge(x, (y_rwkv, y_mla, y_conv, y_mem), wi[:, o_merge:].astype(BF16), b_gate[l],
                   w_o_branch[l].astype(BF16), w_out[l].astype(BF16), row(ln_g[l]), row(ln_b[l]), alpha, ts)
    return x
```

```python
import functools

import jax
import jax.numpy as jnp
from jax import lax
from jax.experimental import pallas as pl
from jax.experimental.pallas import tpu as pltpu

F32 = jnp.float32
BF16 = jnp.bfloat16

N_BRANCH = 4
RWKV_HEAD_DIM = 64
RWKV_LORA = 64
RWKV_GN_EPS = 64e-5
MLA_NOPE = 64
MLA_ROPE = 32
MLA_V = 64
ROPE_THETA = 10000.0
CONV_KERNEL = 31
XATTN_HEADS = 4
LN_EPS = 1e-5
RMS_EPS = 1e-6

LANE = 128
SUBLANE = 8
MXU_WIDTH = 256
CHUNK = 64
CONV_HALO = 32
ATT_TK = 128
LOG2_E = 1.4426950408889634
VMEM_LIMIT = 56 * 1024 * 1024


def _dot(a, b):
    return jnp.dot(a.astype(BF16), b.astype(BF16), preferred_element_type=F32)


def _dot_nt(a, b):
    return lax.dot_general(a.astype(BF16), b.astype(BF16), (((1,), (1,)), ((), ())),
                           preferred_element_type=F32)


def _split_hi_lo(a):
    hi = a.astype(BF16)
    lo = (a - hi.astype(F32)).astype(BF16)
    return hi, lo


def _dot_0_1_rhs(a, ones_like_matrix):
    hi, lo = _split_hi_lo(a)
    return (jnp.dot(hi, ones_like_matrix, preferred_element_type=F32)
            + jnp.dot(lo, ones_like_matrix, preferred_element_type=F32))


def _head_sums(a, seg):
    g = seg.shape[0]
    return jnp.concatenate([_dot_0_1_rhs(a[:, o:o + g], seg) for o in range(0, a.shape[1], g)], axis=1)


def _sigmoid(x):
    return 1.0 / (1.0 + jnp.exp(-x))


def _silu(x):
    return x * _sigmoid(x)


def _const_spec(shape):
    nd = len(shape)
    return pl.BlockSpec(shape, lambda *_: (0,) * nd)


def _params(*sem):
    return pltpu.CompilerParams(dimension_semantics=sem, vmem_limit_bytes=VMEM_LIMIT)


def _rope_kernel(pos_ref, f_ref, cos_ref, sin_ref):
    ang = pos_ref[0].astype(F32) * f_ref[...]
    cos_ref[0] = jnp.cos(ang)
    sin_ref[0] = jnp.sin(ang)


def _rope_tables(positions, freq128):
    B, S = positions.shape
    return pl.pallas_call(
        _rope_kernel,
        grid=(B,),
        in_specs=[pl.BlockSpec((1, S, 1), lambda b: (b, 0, 0)), _const_spec((1, LANE))],
        out_specs=[pl.BlockSpec((1, S, LANE), lambda b: (b, 0, 0))] * 2,
        out_shape=[jax.ShapeDtypeStruct((B, S, LANE), F32)] * 2,
        compiler_params=_params("arbitrary"),
        name="rope_tables",
    )(positions[..., None], freq128)


def _rwkv_kernel(x_ref, w_ref, mu_ref, w0_ref, w2_ref, a0_ref, a2_ref, kk_ref, ka_ref, rk_ref,
                 lng_ref, lnb_ref, seg_ref, o_ref,
                 carry_ref, z_ref, r_s, k_s, v_s, g_s, at_s, rt_s, bt_s, kt_s, bh_s, kh_s, et_s, y_s, *, ts, width):
    W = width
    P = 3 * W + 2 * RWKV_LORA
    n_pairs = W // LANE
    s_idx = pl.program_id(1)

    @pl.when(s_idx == 0)
    def _():
        carry_ref[...] = jnp.zeros_like(carry_ref)
        z_ref[...] = jnp.zeros_like(z_ref)

    h = jnp.dot(x_ref[0].astype(BF16), w_ref[...], preferred_element_type=F32)
    hp = h[:, :P]
    rows = lax.broadcasted_iota(jnp.int32, hp.shape, 0)
    prev = jnp.where(rows == 0, carry_ref[0:1, :], pltpu.roll(hp, 1, 0))
    carry_ref[0:1, :] = hp[ts - 1:ts, :]
    p = hp + mu_ref[...] * (prev - hp)
    r, k, v, wa = p[:, :W], p[:, W:2 * W], p[:, 2 * W:3 * W], p[:, 3 * W:]

    zz = -(w0_ref[...] + _dot(jnp.tanh(wa), w2_ref[...]))
    softplus = jnp.maximum(zz, 0.0) + jnp.log(1.0 + jnp.exp(-jnp.abs(zz)))
    lw = -jnp.exp(-softplus - 0.5)
    a = _sigmoid(a0_ref[...] + _dot(wa, a2_ref[...]))
    kk = k * kk_ref[...]
    ss = _head_sums(kk * kk, seg_ref[...])
    kkn = kk * lax.rsqrt(jnp.maximum(ss, 1e-24))
    kmod = k * (1.0 + (a - 1.0) * ka_ref[...])
    b = kkn * a

    C = CHUNK
    n_chunks = ts // C
    rr = lax.broadcasted_iota(jnp.int32, (ts, ts), 0)
    cc = lax.broadcasted_iota(jnp.int32, (ts, ts), 1)
    same_chunk = (rr // C) == (cc // C)
    lw_hi, lw_lo = _split_hi_lo(lw)
    tri = jnp.where(same_chunk & (rr >= cc), 1.0, 0.0).astype(BF16)
    blk = jnp.where(same_chunk, 1.0, 0.0).astype(BF16)
    cs = jnp.dot(tri, lw_hi, preferred_element_type=F32) + jnp.dot(tri, lw_lo, preferred_element_type=F32)
    tot = jnp.dot(blk, lw_hi, preferred_element_type=F32) + jnp.dot(blk, lw_lo, preferred_element_type=F32)
    w_inv, w_end = jnp.exp(-cs), jnp.exp(tot - cs)
    r_s[...] = r
    k_s[...] = kmod
    v_s[...] = v
    g_s[...] = _silu(h[:, P:])
    at_s[...] = -kkn * jnp.exp(cs - lw)
    rt_s[...] = r * jnp.exp(cs)
    bt_s[...] = b * w_inv
    kt_s[...] = kmod * w_inv
    bh_s[...] = b * w_end
    kh_s[...] = kmod * w_end
    et_s[...] = jnp.exp(tot)

    units = [(c, pr) for c in range(n_chunks) for pr in range(n_pairs)]
    first_head = lax.broadcasted_iota(jnp.int32, (C, LANE), 1) < RWKV_HEAD_DIM

    def stacked(ref):
        out = []
        for c, pr in units:
            t = ref[c * C:(c + 1) * C, pr * LANE:(pr + 1) * LANE]
            out.append(jnp.concatenate([jnp.where(first_head, t, 0.0), jnp.where(first_head, 0.0, t)],
                                       axis=0).astype(BF16))
        return jnp.stack(out)

    def bmm(x, y):
        return jnp.einsum('umk,ukn->umn', x.astype(BF16), y.astype(BF16), preferred_element_type=F32)

    def bmm_nt(x, y):
        return jnp.einsum('umk,unk->umn', x.astype(BF16), y.astype(BF16), preferred_element_type=F32)

    def bmm_tn(x, y):
        return jnp.einsum('ukm,ukn->umn', x.astype(BF16), y.astype(BF16), preferred_element_type=F32)

    ri = lax.broadcasted_iota(jnp.int32, (2 * C, 2 * C), 0)
    ci = lax.broadcasted_iota(jnp.int32, (2 * C, 2 * C), 1)
    strict = (ri > ci)[None]
    incl = (ri >= ci)[None]
    eye = (ri == ci).astype(F32)[None]

    As, Rs, Bs, Ks = stacked(at_s), stacked(rt_s), stacked(bt_s), stacked(kt_s)
    Vs, Bh, Kh = stacked(v_s), stacked(bh_s), stacked(kh_s)
    G = bmm_nt(jnp.concatenate([As, Rs], axis=1), jnp.concatenate([Bs, Ks], axis=1))
    L = jnp.where(strict, G[:, :2 * C, :2 * C], 0.0)
    Lak = jnp.where(strict, G[:, :2 * C, 2 * C:], 0.0)
    Mrb = jnp.where(incl, G[:, 2 * C:, :2 * C], 0.0)
    Mrk = jnp.where(incl, G[:, 2 * C:, 2 * C:], 0.0)
    T = eye + L
    Lb = L.astype(BF16)
    Pw = bmm(Lb, Lb)
    step = 2
    while step * 2 < C:
        Pb = Pw.astype(BF16)
        R2 = bmm(Pb, jnp.concatenate([Pb, T.astype(BF16)], axis=2))
        Pw, T = R2[:, :, :2 * C], T + R2[:, :, 2 * C:]
        step *= 2
    T = T + bmm(Pw, T)
    X = bmm(Lak, Vs)
    AU = bmm(T, jnp.concatenate([As, X.astype(BF16)], axis=2))
    Ah, U0 = AU[:, :, :LANE], AU[:, :, LANE:]
    Y0 = bmm(Mrk, Vs)
    GH = bmm_tn(AU, Bh)
    Gc = GH[:, :LANE]
    Hc = GH[:, LANE:] + bmm_tn(Vs, Kh)
    Z = z_ref[...]
    starts = []
    for c in range(n_chunks):
        us = slice(c * n_pairs, (c + 1) * n_pairs)
        e_tot = jnp.stack([et_s[c * C:c * C + 1, pr * LANE:(pr + 1) * LANE] for pr in range(n_pairs)])
        starts.append(Z)
        Z = Z * e_tot + bmm(Z, Gc[us]) + Hc[us]
    z_ref[...] = Z
    Z0 = jnp.concatenate(starts, axis=0)
    AR = bmm_nt(jnp.concatenate([Ah.astype(BF16), Rs], axis=1), Z0)
    Uu = AR[:, :2 * C] + U0
    Ys = AR[:, 2 * C:] + bmm(Mrb, Uu) + Y0
    ysum = Ys[:, :C] + Ys[:, C:]
    for u, (c, pr) in enumerate(units):
        y_s[c * C:(c + 1) * C, pr * LANE:(pr + 1) * LANE] = ysum[u]

    y = y_s[...]
    inv_n = 1.0 / RWKV_HEAD_DIM
    y_mu = _head_sums(y, seg_ref[...]) * inv_n
    d = y - y_mu
    y_var = _head_sums(d * d, seg_ref[...]) * inv_n
    yn = d * lax.rsqrt(y_var + RWKV_GN_EPS) * lng_ref[...] + lnb_ref[...]
    bonus = _head_sums(r_s[...] * k_s[...] * rk_ref[...], seg_ref[...]) * v_s[...]
    o_ref[0] = ((yn + bonus) * g_s[...]).astype(o_ref.dtype)


def _rwkv_branch(x, w_rw, mu, w0, w2p, a0, a2p, k_k, k_a, r_k, lnx_g, lnx_b, seg, ts):
    B, S, D = x.shape
    W = w0.shape[-1]
    P = 3 * W + 2 * RWKV_LORA
    n_pairs = W // LANE
    row = lambda n: _const_spec((1, n))
    scr = lambda: pltpu.VMEM((ts, W), F32)
    return pl.pallas_call(
        functools.partial(_rwkv_kernel, ts=ts, width=W),
        grid=(B, S // ts),
        in_specs=[pl.BlockSpec((1, ts, D), lambda b, s: (b, s, 0)),
                  _const_spec(w_rw.shape), row(P), row(W), _const_spec(w2p.shape), row(W),
                  _const_spec(a2p.shape), row(W), row(W), row(W), row(W), row(W), _const_spec(seg.shape)],
        out_specs=pl.BlockSpec((1, ts, W), lambda b, s: (b, s, 0)),
        out_shape=jax.ShapeDtypeStruct((B, S, W), BF16),
        scratch_shapes=[pltpu.VMEM((8, P), F32), pltpu.VMEM((n_pairs, LANE, LANE), F32),
                        scr(), scr(), scr(), scr(), scr(), scr(), scr(), scr(), scr(), scr(), scr(), scr()],
        compiler_params=_params("arbitrary", "arbitrary"),
        name="rwkv_branch",
    )(x, w_rw, mu, w0, w2p, a0, a2p, k_k, k_a, r_k, lnx_g, lnx_b, seg)


def _mla_prep_kernel(x_ref, w_ref, qn_ref, kvn_ref, wq_ref, wkv_ref, cos_ref, sin_ref,
                     q_ref, k_ref, vt_ref, g_ref, *, q_lora, kv_lora, heads, scale):
    h = jnp.dot(x_ref[0].astype(BF16), w_ref[...], preferred_element_type=F32)
    ql = h[:, :q_lora]
    kvl = h[:, q_lora:q_lora + kv_lora]
    o = q_lora + kv_lora
    kpe, kpe_rot = h[:, o:o + LANE], h[:, o + LANE:o + 2 * LANE]
    gate = h[:, o + 2 * LANE:]
    cos, sin = cos_ref[0], sin_ref[0]
    qn = ql * lax.rsqrt(jnp.mean(ql * ql, axis=-1, keepdims=True) + RMS_EPS) * qn_ref[...]
    kvn = kvl * lax.rsqrt(jnp.mean(kvl * kvl, axis=-1, keepdims=True) + RMS_EPS) * kvn_ref[...]
    q2 = _dot(qn, wq_ref[...])
    kv = _dot(kvn, wkv_ref[...])
    hw = heads * LANE
    k_pe = kpe * cos + kpe_rot * sin
    ones_lane = jnp.where(lax.broadcasted_iota(jnp.int32, (1, LANE), 1) == MLA_V, 1.0, 0.0)
    ts = h.shape[0]
    for i in range(heads):
        ls = slice(i * LANE, (i + 1) * LANE)
        q_ref[0, :, ls] = ((q2[:, ls] * cos + q2[:, hw + i * LANE:hw + (i + 1) * LANE] * sin) * scale).astype(BF16)
        k_ref[0, :, ls] = (kv[:, ls] + k_pe).astype(BF16)
        v_t = (kv[:, hw + i * LANE:hw + (i + 1) * LANE] + ones_lane).T
        for t in range(ts // ATT_TK):
            vt_ref[0, i, t] = v_t[:, t * ATT_TK:(t + 1) * ATT_TK].astype(BF16)
    g_ref[0] = _silu(gate).astype(BF16)


def _mla_prep(x, w_mla, q_norm, kv_norm, wq2, wkv2, cos, sin, heads, ts):
    B, S, D = x.shape
    q_lora, kv_lora = q_norm.shape[-1], kv_norm.shape[-1]
    W = heads * MLA_V
    tile = lambda n: pl.BlockSpec((1, ts, n), lambda b, s: (b, s, 0))
    return pl.pallas_call(
        functools.partial(_mla_prep_kernel, q_lora=q_lora, kv_lora=kv_lora, heads=heads,
                          scale=float(MLA_NOPE + MLA_ROPE) ** -0.5 * LOG2_E),
        grid=(B, S // ts),
        in_specs=[tile(D), _const_spec(w_mla.shape), _const_spec((1, q_lora)), _const_spec((1, kv_lora)),
                  _const_spec(wq2.shape), _const_spec(wkv2.shape), tile(LANE), tile(LANE)],
        out_specs=[tile(heads * LANE), tile(heads * LANE),
                   pl.BlockSpec((1, heads, ts // ATT_TK, LANE, ATT_TK), lambda b, s: (b, 0, s, 0, 0)), tile(W)],
        out_shape=[jax.ShapeDtypeStruct((B, S, heads * LANE), BF16)] * 2
        + [jax.ShapeDtypeStruct((B, heads, S // ATT_TK, LANE, ATT_TK), BF16), jax.ShapeDtypeStruct((B, S, W), BF16)],
        compiler_params=_params("arbitrary", "arbitrary"),
        name="mla_prep",
    )(x, w_mla, q_norm, kv_norm, wq2, wkv2, cos, sin)


def _mla_attn_kernel(q_ref, k_ref, vt_ref, g_ref, o_ref, m_s, acc_s, s_s, *, tq, qb):
    i = pl.program_id(2)
    tk = ATT_TK
    n_blk = tq // qb
    streams = [(e, hb) for e in range(2) for hb in range(n_blk)]
    key_i = lax.broadcasted_iota(jnp.int32, (tk, qb), 0)
    qry_i = lax.broadcasted_iota(jnp.int32, (tk, qb), 1)

    def scores(j, mode, buf):
        rows_k = pl.ds(pl.multiple_of(j * tk, tk), tk)
        for idx, (e, hb) in enumerate(streams):
            if mode[hb] is not None:
                ls = slice(e * LANE, (e + 1) * LANE)
                s_s[buf, idx] = lax.dot_general(k_ref[0, rows_k, ls], q_ref[0, hb * qb:(hb + 1) * qb, ls],
                                                (((1,), (1,)), ((), ())), preferred_element_type=F32)

    def consume(j, mode, buf):
        for idx, (e, hb) in enumerate(streams):
            if mode[hb] is None:
                continue
            s = s_s[buf, idx]
            if mode[hb] is not False:
                s = jnp.where(key_i + mode[hb] <= qry_i, s, -1e30)
            m = m_s[idx, 0:1, :]
            m_new = jnp.maximum(m, jnp.max(s, axis=0, keepdims=True))
            p = jnp.exp2(s - m_new)
            acc_s[idx] = jnp.exp2(m - m_new) * acc_s[idx] + jnp.dot(vt_ref[0, e, j], p.astype(BF16),
                                                                    preferred_element_type=F32)
            m_s[idx, 0:1, :] = m_new

    tiles_per_blk = qb // tk
    n_diag = tq // tk
    assert n_diag % 2 == 0
    first_diag = i * n_diag
    visible = [False] * n_blk

    def diag_mode(d):
        mode = []
        for hb in range(n_blk):
            rel = d - hb * tiles_per_blk
            mode.append(False if rel < 0 else (None if rel >= tiles_per_blk else rel * tk))
        return mode

    m_s[...] = jnp.full(m_s.shape, -1e30, F32)
    acc_s[...] = jnp.zeros(acc_s.shape, F32)
    scores(0, visible, 0)

    def body(jj, carry):
        j = 2 * jj
        scores(j + 1, visible, 1)
        consume(j, visible, 0)
        scores(j + 2, visible, 0)
        consume(j + 1, visible, 1)
        return carry

    lax.fori_loop(0, first_diag // 2, body, 0)
    for d in range(n_diag):
        if d + 1 < n_diag:
            scores(first_diag + d + 1, diag_mode(d + 1), (d + 1) % 2)
        consume(first_diag + d, diag_mode(d), d % 2)

    for hb in range(n_blk):
        rows = slice(hb * qb, (hb + 1) * qb)
        a0, a1 = acc_s[streams.index((0, hb))], acc_s[streams.index((1, hb))]
        o_t = jnp.concatenate([a0[:MLA_V] / a0[MLA_V:MLA_V + 1], a1[:MLA_V] / a1[MLA_V:MLA_V + 1]], axis=0)
        o_ref[0, rows, :] = (o_t.T * g_ref[0, rows, :].astype(F32)).astype(o_ref.dtype)


def _mla_attention(q, k, vt, g, tq, qb):
    B, S, HW = q.shape
    W = g.shape[-1]
    n_pairs = W // LANE
    n_streams = 2 * (tq // qb)
    return pl.pallas_call(
        functools.partial(_mla_attn_kernel, tq=tq, qb=qb),
        grid=(B, n_pairs, S // tq),
        in_specs=[pl.BlockSpec((1, tq, 2 * LANE), lambda b, p, i: (b, i, p)),
                  pl.BlockSpec((1, S, 2 * LANE), lambda b, p, i: (b, 0, p)),
                  pl.BlockSpec((1, 2, S // ATT_TK, LANE, ATT_TK), lambda b, p, i: (b, p, 0, 0, 0)),
                  pl.BlockSpec((1, tq, LANE), lambda b, p, i: (b, i, p))],
        out_specs=pl.BlockSpec((1, tq, LANE), lambda b, p, i: (b, i, p)),
        out_shape=jax.ShapeDtypeStruct((B, S, W), BF16),
        scratch_shapes=[pltpu.VMEM((n_streams, SUBLANE, qb), F32), pltpu.VMEM((n_streams, LANE, qb), F32),
                        pltpu.VMEM((2, n_streams, ATT_TK, qb), F32)],
        compiler_params=_params("arbitrary", "arbitrary", "arbitrary"),
        name="mla_attention",
    )(q, k, vt, g)


def _conv_kernel(x_ref, w_ref, cw_ref, cb_ref, lg_ref, lb_ref, o_ref, hbuf, *, ts, width, row_block):
    W = width
    s_idx = pl.program_id(1)

    @pl.when(s_idx == 0)
    def _():
        hbuf[0, 0:CONV_HALO, :] = jnp.zeros((CONV_HALO, W), F32)

    @pl.when(s_idx > 0)
    def _():
        hbuf[0, 0:CONV_HALO, :] = hbuf[0, ts:ts + CONV_HALO, :]

    h = jnp.dot(x_ref[0].astype(BF16), w_ref[...], preferred_element_type=F32)
    hbuf[0, CONV_HALO:CONV_HALO + ts, :] = h[:, :W] * _sigmoid(h[:, W:2 * W])
    shifted_rows = ts + CONV_HALO - SUBLANE
    for r in range(1, SUBLANE):
        hbuf[r, 0:shifted_rows, :] = hbuf[0, r:r + shifted_rows, :]
    first_tap = CONV_HALO - (CONV_KERNEL - 1)
    for rb in range(ts // row_block):
        r0 = rb * row_block
        acc = jnp.zeros((row_block, W), F32) + cb_ref[...]
        for j in range(CONV_KERNEL):
            q8, r = divmod(first_tap + j, SUBLANE)
            acc = acc + cw_ref[j:j + 1, :] * hbuf[r, r0 + q8 * SUBLANE:r0 + q8 * SUBLANE + row_block, :]
        mu = jnp.mean(acc, axis=-1, keepdims=True)
        d = acc - mu
        var = jnp.mean(d * d, axis=-1, keepdims=True)
        y = _silu(d * lax.rsqrt(var + LN_EPS) * lg_ref[...] + lb_ref[...])
        o_ref[0, r0:r0 + row_block, :] = (y * _silu(h[r0:r0 + row_block, 2 * W:])).astype(o_ref.dtype)


def _conv_branch(x, w_conv, conv_w, conv_b, ln_g, ln_b, ts):
    B, S, D = x.shape
    W = conv_b.shape[-1]
    return pl.pallas_call(
        functools.partial(_conv_kernel, ts=ts, width=W, row_block=32),
        grid=(B, S // ts),
        in_specs=[pl.BlockSpec((1, ts, D), lambda b, s: (b, s, 0)), _const_spec(w_conv.shape),
                  _const_spec(conv_w.shape), _const_spec((1, W)), _const_spec((1, W)), _const_spec((1, W))],
        out_specs=pl.BlockSpec((1, ts, W), lambda b, s: (b, s, 0)),
        out_shape=jax.ShapeDtypeStruct((B, S, W), BF16),
        scratch_shapes=[pltpu.VMEM((SUBLANE, ts + CONV_HALO, W), F32)],
        compiler_params=_params("arbitrary", "arbitrary"),
        name="conv_branch",
    )(x, w_conv, conv_w, conv_b, ln_g, ln_b)


def _xattn_kernel(x_ref, mem_ref, w_ref, wkv_ref, o_ref, kv_s, *, width, scale):
    W = width

    @pl.when(pl.program_id(1) == 0)
    def _():
        kv_s[...] = jnp.dot(mem_ref[0].astype(BF16), wkv_ref[...], preferred_element_type=F32).astype(BF16)

    h = jnp.dot(x_ref[0].astype(BF16), w_ref[...], preferred_element_type=F32)
    for i in range(XATTN_HEADS):
        ls = slice(i * LANE, (i + 1) * LANE)
        s = _dot_nt(h[:, ls] * scale, kv_s[:, ls])
        p = jnp.exp(s - jnp.max(s, axis=-1, keepdims=True))
        o = _dot(p, kv_s[:, W + i * LANE:W + (i + 1) * LANE]) / jnp.sum(p, axis=-1, keepdims=True)
        o_ref[0, :, ls] = (o * _silu(h[:, W + i * LANE:W + (i + 1) * LANE])).astype(o_ref.dtype)


def _xattn_branch(x, mem, w_x, w_mem_kv, ts):
    B, S, D = x.shape
    M = mem.shape[1]
    W = w_x.shape[-1] // 2
    return pl.pallas_call(
        functools.partial(_xattn_kernel, width=W, scale=float(W // XATTN_HEADS) ** -0.5),
        grid=(B, S // ts),
        in_specs=[pl.BlockSpec((1, ts, D), lambda b, s: (b, s, 0)), pl.BlockSpec((1, M, D), lambda b, s: (b, 0, 0)),
                  _const_spec(w_x.shape), _const_spec(w_mem_kv.shape)],
        out_specs=pl.BlockSpec((1, ts, W), lambda b, s: (b, s, 0)),
        out_shape=jax.ShapeDtypeStruct((B, S, W), BF16),
        scratch_shapes=[pltpu.VMEM((M, 2 * W), BF16)],
        compiler_params=_params("arbitrary", "arbitrary"),
        name="xattn_branch",
    )(x, mem, w_x, w_mem_kv)


def _merge_kernel(x_ref, y0_ref, y1_ref, y2_ref, y3_ref, wm_ref, bg_ref, wo_ref, wout_ref, lg_ref, lb_ref,
                  o_ref, *, alpha):
    x = x_ref[0]
    xb = x.astype(BF16)
    D = x.shape[-1]
    merged = None
    for n, y_ref in enumerate((y0_ref, y1_ref, y2_ref, y3_ref)):
        gate = _sigmoid(jnp.dot(xb, wm_ref[:, n * D:(n + 1) * D], preferred_element_type=F32) + bg_ref[n:n + 1, :])
        term = gate * jnp.dot(y_ref[0], wo_ref[n], preferred_element_type=F32)
        merged = term if merged is None else merged + term
    z = alpha * x + _dot(merged, wout_ref[...])
    mu = jnp.mean(z, axis=-1, keepdims=True)
    d = z - mu
    var = jnp.mean(d * d, axis=-1, keepdims=True)
    o_ref[0] = d * lax.rsqrt(var + LN_EPS) * lg_ref[...] + lb_ref[...]


def _merge(x, ys, w_merge, b_gate, w_o, w_out, ln_g, ln_b, alpha, ts):
    B, S, D = x.shape
    W = ys[0].shape[-1]
    tile = lambda n: pl.BlockSpec((1, ts, n), lambda b, s: (b, s, 0))
    return pl.pallas_call(
        functools.partial(_merge_kernel, alpha=alpha),
        grid=(B, S // ts),
        in_specs=[tile(D), tile(W), tile(W), tile(W), tile(W), _const_spec(w_merge.shape),
                  _const_spec(b_gate.shape), _const_spec(w_o.shape), _const_spec(w_out.shape),
                  _const_spec((1, D)), _const_spec((1, D))],
        out_specs=tile(D),
        out_shape=jax.ShapeDtypeStruct((B, S, D), F32),
        compiler_params=_params("arbitrary", "arbitrary"),
        name="merge_out",
    )(x, *ys, w_merge, b_gate, w_o, w_out, ln_g, ln_b)


def _rotate_half_cols(w):
    half = w.shape[-1] // 2
    return jnp.concatenate([-w[..., half:], w[..., :half]], axis=-1)


def kernel(x, mem, positions, w_in, b_gate, rwkv_mu, rwkv_w0, rwkv_w2, rwkv_a0, rwkv_a2, rwkv_k_k, rwkv_k_a, rwkv_r_k, rwkv_lnx_g, rwkv_lnx_b, mla_q_norm, mla_w_uq, mla_kv_norm, mla_w_ukv, conv_w, conv_b, conv_ln_g, conv_ln_b, xattn_w_mem_kv, w_o_branch, w_out, ln_g, ln_b):
    B, S, D = x.shape
    depth = w_in.shape[0]
    W = D // 2
    q_lora, kv_lora = mla_q_norm.shape[-1], mla_kv_norm.shape[-1]
    heads = W // MLA_V
    P = 3 * W + 2 * RWKV_LORA
    alpha = (2.0 * depth) ** 0.25
    assert W % LANE == 0 and LANE == MLA_NOPE + MLA_V and MLA_NOPE + MLA_ROPE <= LANE
    ts = min(S, 256)
    qb = min(S, 256)
    tq = min(S, 2 * qb)
    assert S % ts == 0 and ts % CHUNK == 0 and ts % ATT_TK == 0 and S % tq == 0 and tq % qb == 0 and qb % ATT_TK == 0

    sizes = (P, W, q_lora, kv_lora, MLA_ROPE, W, 2 * W, W, W, W, N_BRANCH * D)
    offs = [0]
    for n in sizes:
        offs.append(offs[-1] + n)
    o_rw, _, o_q, o_kv, o_kpe, o_mg, o_cu, _, o_xq, _, o_merge, o_end = offs
    assert o_end == w_in.shape[-1]

    inv_freq = ROPE_THETA ** (-jnp.arange(0, MLA_ROPE, 2, dtype=F32) / MLA_ROPE)
    zeros_f = lambda n: jnp.zeros((n,), F32)
    freq128 = jnp.concatenate([zeros_f(MLA_NOPE), inv_freq, inv_freq, zeros_f(LANE - MLA_NOPE - MLA_ROPE)])[None, :]
    cos, sin = _rope_tables(positions, freq128)

    lane_head = jnp.arange(MXU_WIDTH) // RWKV_HEAD_DIM
    seg = (lane_head[:, None] == lane_head[None, :]).astype(BF16)
    zl = jnp.zeros((RWKV_LORA, W), F32)
    row = lambda a: a.reshape(1, -1)

    for l in range(depth):
        wi = w_in[l]
        w_rw = wi[:, o_rw:o_q].astype(BF16)
        w2p = jnp.concatenate([rwkv_w2[l], zl], axis=0).astype(BF16)
        a2p = jnp.concatenate([zl, rwkv_a2[l]], axis=0).astype(BF16)
        y_rwkv = _rwkv_branch(x, w_rw, row(rwkv_mu[l]), row(rwkv_w0[l]), w2p, row(rwkv_a0[l]), a2p,
                              row(rwkv_k_k[l]), row(rwkv_k_a[l]), row(rwkv_r_k[l]), row(rwkv_lnx_g[l]),
                              row(rwkv_lnx_b[l]), seg, ts)

        kpe_w = wi[:, o_kpe:o_mg]
        pad_l, pad_r = jnp.zeros((D, MLA_NOPE), F32), jnp.zeros((D, LANE - MLA_NOPE - MLA_ROPE), F32)
        w_mla = jnp.concatenate([wi[:, o_q:o_kpe], pad_l, kpe_w, pad_r, pad_l, _rotate_half_cols(kpe_w), pad_r,
                                 wi[:, o_mg:o_cu]], axis=1).astype(BF16)
        wq = mla_w_uq[l].reshape(q_lora, heads, MLA_NOPE + MLA_ROPE)
        q_nope, q_pe = wq[..., :MLA_NOPE], wq[..., MLA_NOPE:]
        zq = lambda n: jnp.zeros((q_lora, heads, n), F32)
        wq_plain = jnp.concatenate([q_nope, q_pe, zq(LANE - MLA_NOPE - MLA_ROPE)], axis=-1)
        wq_rot = jnp.concatenate([zq(MLA_NOPE), _rotate_half_cols(q_pe), zq(LANE - MLA_NOPE - MLA_ROPE)], axis=-1)
        wq2 = jnp.concatenate([wq_plain.reshape(q_lora, -1), wq_rot.reshape(q_lora, -1)], axis=1).astype(BF16)
        wkv = mla_w_ukv[l].reshape(kv_lora, heads, MLA_NOPE + MLA_V)
        wk = jnp.concatenate([wkv[..., :MLA_NOPE], jnp.zeros((kv_lora, heads, LANE - MLA_NOPE), F32)], axis=-1)
        wv = jnp.concatenate([wkv[..., MLA_NOPE:], jnp.zeros((kv_lora, heads, LANE - MLA_V), F32)], axis=-1)
        wkv2 = jnp.concatenate([wk.reshape(kv_lora, -1), wv.reshape(kv_lora, -1)], axis=1).astype(BF16)
        q, k, vt, g = _mla_prep(x, w_mla, row(mla_q_norm[l]), row(mla_kv_norm[l]), wq2, wkv2, cos, sin, heads, ts)
        y_mla = _mla_attention(q, k, vt, g, tq, qb)

        y_conv = _conv_branch(x, wi[:, o_cu:o_xq].astype(BF16), conv_w[l], row(conv_b[l]),
                              row(conv_ln_g[l]), row(conv_ln_b[l]), ts)
        y_mem = _xattn_branch(x, mem, wi[:, o_xq:o_merge].astype(BF16), xattn_w_mem_kv[l].astype(BF16), ts)

        x = _merge(x, (y_rwkv, y_mla, y_conv, y_mem), wi[:, o_merge:].astype(BF16), b_gate[l],
                   w_o_branch[l].astype(BF16), w_out[l].astype(BF16), row(ln_g[l]), row(ln_b[l]), alpha, ts)
    return x
```

```python
import functools

import jax
import jax.numpy as jnp
from jax import lax
from jax.experimental import pallas as pl
from jax.experimental.pallas import tpu as pltpu

F32 = jnp.float32
BF16 = jnp.bfloat16

N_BRANCH = 4
RWKV_HEAD_DIM = 64
RWKV_LORA = 64
RWKV_GN_EPS = 64e-5
MLA_NOPE = 64
MLA_ROPE = 32
MLA_V = 64
ROPE_THETA = 10000.0
CONV_KERNEL = 31
XATTN_HEADS = 4
LN_EPS = 1e-5
RMS_EPS = 1e-6

LANE = 128
SUBLANE = 8
MXU_WIDTH = 256
CHUNK = 64
CONV_HALO = 32
ATT_TK = 128
ATT_VROWS = 80
LOG2_E = 1.4426950408889634
VMEM_LIMIT = 56 * 1024 * 1024


def _dot(a, b):
    return jnp.dot(a.astype(BF16), b.astype(BF16), preferred_element_type=F32)


def _dot_nt(a, b):
    return lax.dot_general(a.astype(BF16), b.astype(BF16), (((1,), (1,)), ((), ())),
                           preferred_element_type=F32)


def _split_hi_lo(a):
    hi = a.astype(BF16)
    lo = (a - hi.astype(F32)).astype(BF16)
    return hi, lo


def _dot_0_1_rhs(a, ones_like_matrix):
    hi, lo = _split_hi_lo(a)
    return (jnp.dot(hi, ones_like_matrix, preferred_element_type=F32)
            + jnp.dot(lo, ones_like_matrix, preferred_element_type=F32))


def _head_sums(a, seg):
    g = seg.shape[0]
    return jnp.concatenate([_dot_0_1_rhs(a[:, o:o + g], seg) for o in range(0, a.shape[1], g)], axis=1)


def _sigmoid(x):
    return 1.0 / (1.0 + jnp.exp(-x))


def _silu(x):
    return x * _sigmoid(x)


def _const_spec(shape, single_buffer=False):
    nd = len(shape)
    mode = pl.Buffered(1) if single_buffer else None
    return pl.BlockSpec(shape, lambda *_: (0,) * nd, pipeline_mode=mode)


def _params(*sem):
    return pltpu.CompilerParams(dimension_semantics=sem, vmem_limit_bytes=VMEM_LIMIT)


def _rope_kernel(pos_ref, f_ref, cos_ref, sin_ref):
    ang = pos_ref[0].astype(F32) * f_ref[...]
    cos_ref[0] = jnp.cos(ang)
    sin_ref[0] = jnp.sin(ang)


def _rope_tables(positions, freq128):
    B, S = positions.shape
    return pl.pallas_call(
        _rope_kernel,
        grid=(B,),
        in_specs=[pl.BlockSpec((1, S, 1), lambda b: (b, 0, 0)), _const_spec((1, LANE))],
        out_specs=[pl.BlockSpec((1, S, LANE), lambda b: (b, 0, 0))] * 2,
        out_shape=[jax.ShapeDtypeStruct((B, S, LANE), F32)] * 2,
        compiler_params=_params("arbitrary"),
        name="rope_tables",
    )(positions[..., None], freq128)


def _rwkv_kernel(x_ref, w_ref, mu_ref, w0_ref, w2_ref, a0_ref, a2_ref, kk_ref, ka_ref, rk_ref,
                 lng_ref, lnb_ref, seg_ref, o_ref,
                 carry_ref, z_ref, r_s, k_s, v_s, g_s, at_s, rt_s, bt_s, kt_s, bh_s, kh_s, et_s, y_s, *, ts, width):
    W = width
    P = 3 * W + 2 * RWKV_LORA
    n_pairs = W // LANE
    s_idx = pl.program_id(1)

    @pl.when(s_idx == 0)
    def _():
        carry_ref[...] = jnp.zeros_like(carry_ref)
        z_ref[...] = jnp.zeros_like(z_ref)

    h = jnp.dot(x_ref[0].astype(BF16), w_ref[...], preferred_element_type=F32)
    hp = h[:, :P]
    rows = lax.broadcasted_iota(jnp.int32, hp.shape, 0)
    prev = jnp.where(rows == 0, carry_ref[0:1, :], pltpu.roll(hp, 1, 0))
    carry_ref[0:1, :] = hp[ts - 1:ts, :]
    p = hp + mu_ref[...] * (prev - hp)
    r, k, v, wa = p[:, :W], p[:, W:2 * W], p[:, 2 * W:3 * W], p[:, 3 * W:]

    zz = -(w0_ref[...] + _dot(jnp.tanh(wa), w2_ref[...]))
    softplus = jnp.maximum(zz, 0.0) + jnp.log(1.0 + jnp.exp(-jnp.abs(zz)))
    lw = -jnp.exp(-softplus - 0.5)
    a = _sigmoid(a0_ref[...] + _dot(wa, a2_ref[...]))
    kk = k * kk_ref[...]
    ss = _head_sums(kk * kk, seg_ref[...])
    kkn = kk * lax.rsqrt(jnp.maximum(ss, 1e-24))
    kmod = k * (1.0 + (a - 1.0) * ka_ref[...])
    b = kkn * a

    C = CHUNK
    n_chunks = ts // C
    rr = lax.broadcasted_iota(jnp.int32, (ts, ts), 0)
    cc = lax.broadcasted_iota(jnp.int32, (ts, ts), 1)
    same_chunk = (rr // C) == (cc // C)
    lw_hi, lw_lo = _split_hi_lo(lw)
    tri = jnp.where(same_chunk & (rr >= cc), 1.0, 0.0).astype(BF16)
    blk = jnp.where(same_chunk, 1.0, 0.0).astype(BF16)
    cs = jnp.dot(tri, lw_hi, preferred_element_type=F32) + jnp.dot(tri, lw_lo, preferred_element_type=F32)
    tot = jnp.dot(blk, lw_hi, preferred_element_type=F32) + jnp.dot(blk, lw_lo, preferred_element_type=F32)
    w_inv, w_end = jnp.exp(-cs), jnp.exp(tot - cs)
    r_s[...] = r
    k_s[...] = kmod
    v_s[...] = v
    g_s[...] = _silu(h[:, P:])
    at_s[...] = -kkn * jnp.exp(cs - lw)
    rt_s[...] = r * jnp.exp(cs)
    bt_s[...] = b * w_inv
    kt_s[...] = kmod * w_inv
    bh_s[...] = b * w_end
    kh_s[...] = kmod * w_end
    et_s[...] = jnp.exp(tot)

    units = [(c, pr) for c in range(n_chunks) for pr in range(n_pairs)]
    first_head = lax.broadcasted_iota(jnp.int32, (C, LANE), 1) < RWKV_HEAD_DIM

    def stacked(ref):
        out = []
        for c, pr in units:
            t = ref[c * C:(c + 1) * C, pr * LANE:(pr + 1) * LANE]
            out.append(jnp.concatenate([jnp.where(first_head, t, 0.0), jnp.where(first_head, 0.0, t)],
                                       axis=0).astype(BF16))
        return jnp.stack(out)

    def bmm(x, y):
        return jnp.einsum('umk,ukn->umn', x.astype(BF16), y.astype(BF16), preferred_element_type=F32)

    def bmm_nt(x, y):
        return jnp.einsum('umk,unk->umn', x.astype(BF16), y.astype(BF16), preferred_element_type=F32)

    def bmm_tn(x, y):
        return jnp.einsum('ukm,ukn->umn', x.astype(BF16), y.astype(BF16), preferred_element_type=F32)

    ri = lax.broadcasted_iota(jnp.int32, (2 * C, 2 * C), 0)
    ci = lax.broadcasted_iota(jnp.int32, (2 * C, 2 * C), 1)
    strict = (ri > ci)[None]
    incl = (ri >= ci)[None]
    eye = (ri == ci).astype(F32)[None]

    As, Rs, Bs, Ks = stacked(at_s), stacked(rt_s), stacked(bt_s), stacked(kt_s)
    Vs, Bh, Kh = stacked(v_s), stacked(bh_s), stacked(kh_s)
    G = bmm_nt(jnp.concatenate([As, Rs], axis=1), jnp.concatenate([Bs, Ks], axis=1))
    L = jnp.where(strict, G[:, :2 * C, :2 * C], 0.0)
    Lak = jnp.where(strict, G[:, :2 * C, 2 * C:], 0.0)
    Mrb = jnp.where(incl, G[:, 2 * C:, :2 * C], 0.0)
    Mrk = jnp.where(incl, G[:, 2 * C:, 2 * C:], 0.0)
    T = eye + L
    Lb = L.astype(BF16)
    Pw = bmm(Lb, Lb)
    step = 2
    while step * 2 < C:
        Pb = Pw.astype(BF16)
        R2 = bmm(Pb, jnp.concatenate([Pb, T.astype(BF16)], axis=2))
        Pw, T = R2[:, :, :2 * C], T + R2[:, :, 2 * C:]
        step *= 2
    T = T + bmm(Pw, T)
    X = bmm(Lak, Vs)
    AU = bmm(T, jnp.concatenate([As, X.astype(BF16)], axis=2))
    Ah, U0 = AU[:, :, :LANE], AU[:, :, LANE:]
    Gc = bmm_tn(Ah, Bh)
    Hc = bmm_tn(jnp.concatenate([U0.astype(BF16), Vs], axis=1), jnp.concatenate([Bh, Kh], axis=1))
    Z = z_ref[...]
    starts = []
    for c in range(n_chunks):
        us = slice(c * n_pairs, (c + 1) * n_pairs)
        e_tot = jnp.stack([et_s[c * C:c * C + 1, pr * LANE:(pr + 1) * LANE] for pr in range(n_pairs)])
        starts.append(Z)
        Z = Z * e_tot + bmm(Z, Gc[us]) + Hc[us]
    z_ref[...] = Z
    Z0 = jnp.concatenate(starts, axis=0)
    AR = bmm_nt(jnp.concatenate([Ah.astype(BF16), Rs], axis=1), Z0)
    Uu = AR[:, :2 * C] + U0
    Ys = AR[:, 2 * C:] + bmm(jnp.concatenate([Mrb.astype(BF16), Mrk.astype(BF16)], axis=2),
                             jnp.concatenate([Uu.astype(BF16), Vs], axis=1))
    ysum = Ys[:, :C] + Ys[:, C:]
    for u, (c, pr) in enumerate(units):
        y_s[c * C:(c + 1) * C, pr * LANE:(pr + 1) * LANE] = ysum[u]

    y = y_s[...]
    inv_n = 1.0 / RWKV_HEAD_DIM
    y_mu = _head_sums(y, seg_ref[...]) * inv_n
    d = y - y_mu
    y_var = _head_sums(d * d, seg_ref[...]) * inv_n
    yn = d * lax.rsqrt(y_var + RWKV_GN_EPS) * lng_ref[...] + lnb_ref[...]
    bonus = _head_sums(r_s[...] * k_s[...] * rk_ref[...], seg_ref[...]) * v_s[...]
    o_ref[0] = ((yn + bonus) * g_s[...]).astype(o_ref.dtype)


def _rwkv_branch(x, w_rw, mu, w0, w2p, a0, a2p, k_k, k_a, r_k, lnx_g, lnx_b, seg, ts):
    B, S, D = x.shape
    W = w0.shape[-1]
    P = 3 * W + 2 * RWKV_LORA
    n_pairs = W // LANE
    row = lambda n: _const_spec((1, n))
    scr = lambda: pltpu.VMEM((ts, W), F32)
    return pl.pallas_call(
        functools.partial(_rwkv_kernel, ts=ts, width=W),
        grid=(B, S // ts),
        in_specs=[pl.BlockSpec((1, ts, D), lambda b, s: (b, s, 0)),
                  _const_spec(w_rw.shape), row(P), row(W), _const_spec(w2p.shape), row(W),
                  _const_spec(a2p.shape), row(W), row(W), row(W), row(W), row(W), _const_spec(seg.shape)],
        out_specs=pl.BlockSpec((1, ts, W), lambda b, s: (b, s, 0)),
        out_shape=jax.ShapeDtypeStruct((B, S, W), BF16),
        scratch_shapes=[pltpu.VMEM((8, P), F32), pltpu.VMEM((n_pairs, LANE, LANE), F32),
                        scr(), scr(), scr(), scr(), scr(), scr(), scr(), scr(), scr(), scr(), scr(), scr()],
        compiler_params=_params("arbitrary", "arbitrary"),
        name="rwkv_branch",
    )(x, w_rw, mu, w0, w2p, a0, a2p, k_k, k_a, r_k, lnx_g, lnx_b, seg)


def _mla_prep_kernel(x_ref, w_ref, qn_ref, kvn_ref, wq_ref, wkv_ref, cos_ref, sin_ref,
                     q_ref, k_ref, vt_ref, g_ref, *, q_lora, kv_lora, heads, scale):
    h = jnp.dot(x_ref[0].astype(BF16), w_ref[...], preferred_element_type=F32)
    ql = h[:, :q_lora]
    kvl = h[:, q_lora:q_lora + kv_lora]
    o = q_lora + kv_lora
    kpe, kpe_rot = h[:, o:o + LANE], h[:, o + LANE:o + 2 * LANE]
    gate = h[:, o + 2 * LANE:]
    cos, sin = cos_ref[0], sin_ref[0]
    qn = ql * lax.rsqrt(jnp.mean(ql * ql, axis=-1, keepdims=True) + RMS_EPS) * qn_ref[...]
    kvn = kvl * lax.rsqrt(jnp.mean(kvl * kvl, axis=-1, keepdims=True) + RMS_EPS) * kvn_ref[...]
    q2 = _dot(qn, wq_ref[...])
    kv = _dot(kvn, wkv_ref[...])
    hw = heads * LANE
    k_pe = kpe * cos + kpe_rot * sin
    ones_lane = jnp.where(lax.broadcasted_iota(jnp.int32, (1, LANE), 1) == MLA_V, 1.0, 0.0)
    ts = h.shape[0]
    for i in range(heads):
        ls = slice(i * LANE, (i + 1) * LANE)
        q_ref[0, :, ls] = ((q2[:, ls] * cos + q2[:, hw + i * LANE:hw + (i + 1) * LANE] * sin) * scale).astype(BF16)
        k_ref[0, :, ls] = (kv[:, ls] + k_pe).astype(BF16)
        v_t = (kv[:, hw + i * LANE:hw + (i + 1) * LANE] + ones_lane).T
        for t in range(ts // ATT_TK):
            vt_ref[0, i, t] = v_t[:ATT_VROWS, t * ATT_TK:(t + 1) * ATT_TK].astype(BF16)
    g_ref[0] = _silu(gate).astype(BF16)


def _mla_prep(x, w_mla, q_norm, kv_norm, wq2, wkv2, cos, sin, heads, ts):
    B, S, D = x.shape
    q_lora, kv_lora = q_norm.shape[-1], kv_norm.shape[-1]
    W = heads * MLA_V
    tile = lambda n: pl.BlockSpec((1, ts, n), lambda b, s: (b, s, 0))
    return pl.pallas_call(
        functools.partial(_mla_prep_kernel, q_lora=q_lora, kv_lora=kv_lora, heads=heads,
                          scale=float(MLA_NOPE + MLA_ROPE) ** -0.5 * LOG2_E),
        grid=(B, S // ts),
        in_specs=[tile(D), _const_spec(w_mla.shape), _const_spec((1, q_lora)), _const_spec((1, kv_lora)),
                  _const_spec(wq2.shape), _const_spec(wkv2.shape), tile(LANE), tile(LANE)],
        out_specs=[tile(heads * LANE), tile(heads * LANE),
                   pl.BlockSpec((1, heads, ts // ATT_TK, ATT_VROWS, ATT_TK), lambda b, s: (b, 0, s, 0, 0)), tile(W)],
        out_shape=[jax.ShapeDtypeStruct((B, S, heads * LANE), BF16)] * 2
        + [jax.ShapeDtypeStruct((B, heads, S // ATT_TK, ATT_VROWS, ATT_TK), BF16), jax.ShapeDtypeStruct((B, S, W), BF16)],
        compiler_params=_params("arbitrary", "arbitrary"),
        name="mla_prep",
    )(x, w_mla, q_norm, kv_norm, wq2, wkv2, cos, sin)


def _mla_attn_kernel(q_ref, k_ref, vt_ref, g_ref, o_ref, m_s, acc_s, s_s, *, tq, qb):
    i = pl.program_id(2)
    tk = ATT_TK
    n_blk = tq // qb
    streams = [(e, hb) for e in range(2) for hb in range(n_blk)]
    key_i = lax.broadcasted_iota(jnp.int32, (tk, qb), 0)
    qry_i = lax.broadcasted_iota(jnp.int32, (tk, qb), 1)

    def scores(j, mode, buf):
        rows_k = pl.ds(pl.multiple_of(j * tk, tk), tk)
        for idx, (e, hb) in enumerate(streams):
            if mode[hb] is not None:
                ls = slice(e * LANE, (e + 1) * LANE)
                s_s[buf, idx] = lax.dot_general(k_ref[0, rows_k, ls], q_ref[0, hb * qb:(hb + 1) * qb, ls],
                                                (((1,), (1,)), ((), ())), preferred_element_type=F32)

    def consume(j, mode, buf):
        for idx, (e, hb) in enumerate(streams):
            if mode[hb] is None:
                continue
            s = s_s[buf, idx]
            if mode[hb] is not False:
                s = jnp.where(key_i + mode[hb] <= qry_i, s, -1e30)
            m = m_s[idx, 0:1, :]
            m_new = jnp.maximum(m, jnp.max(s, axis=0, keepdims=True))
            p = jnp.exp2(s - m_new)
            acc_s[idx] = jnp.exp2(m - m_new) * acc_s[idx] + jnp.dot(vt_ref[0, e, j], p.astype(BF16),
                                                                    preferred_element_type=F32)
            m_s[idx, 0:1, :] = m_new

    tiles_per_blk = qb // tk
    n_diag = tq // tk
    assert n_diag % 2 == 0
    first_diag = i * n_diag
    visible = [False] * n_blk

    def diag_mode(d):
        mode = []
        for hb in range(n_blk):
            rel = d - hb * tiles_per_blk
            mode.append(False if rel < 0 else (None if rel >= tiles_per_blk else rel * tk))
        return mode

    m_s[...] = jnp.full(m_s.shape, -1e30, F32)
    acc_s[...] = jnp.zeros(acc_s.shape, F32)
    scores(0, visible, 0)

    def body(jj, carry):
        for u in range(n_diag):
            scores(n_diag * jj + u + 1, visible, (u + 1) % 2)
            consume(n_diag * jj + u, visible, u % 2)
        return carry

    lax.fori_loop(0, i, body, 0)
    for d in range(n_diag):
        if d + 1 < n_diag:
            scores(first_diag + d + 1, diag_mode(d + 1), (d + 1) % 2)
        consume(first_diag + d, diag_mode(d), d % 2)

    for hb in range(n_blk):
        rows = slice(hb * qb, (hb + 1) * qb)
        a0, a1 = acc_s[streams.index((0, hb))], acc_s[streams.index((1, hb))]
        o_t = jnp.concatenate([a0[:MLA_V] / a0[MLA_V:MLA_V + 1], a1[:MLA_V] / a1[MLA_V:MLA_V + 1]], axis=0)
        o_ref[0, rows, :] = (o_t.T * g_ref[0, rows, :].astype(F32)).astype(o_ref.dtype)


def _mla_attention(q, k, vt, g, tq, qb):
    B, S, HW = q.shape
    W = g.shape[-1]
    n_pairs = W // LANE
    n_streams = 2 * (tq // qb)
    return pl.pallas_call(
        functools.partial(_mla_attn_kernel, tq=tq, qb=qb),
        grid=(B, n_pairs, S // tq),
        in_specs=[pl.BlockSpec((1, tq, 2 * LANE), lambda b, p, i: (b, i, p)),
                  pl.BlockSpec((1, S, 2 * LANE), lambda b, p, i: (b, 0, p)),
                  pl.BlockSpec((1, 2, S // ATT_TK, ATT_VROWS, ATT_TK), lambda b, p, i: (b, p, 0, 0, 0)),
                  pl.BlockSpec((1, tq, LANE), lambda b, p, i: (b, i, p))],
        out_specs=pl.BlockSpec((1, tq, LANE), lambda b, p, i: (b, i, p)),
        out_shape=jax.ShapeDtypeStruct((B, S, W), BF16),
        scratch_shapes=[pltpu.VMEM((n_streams, SUBLANE, qb), F32), pltpu.VMEM((n_streams, ATT_VROWS, qb), F32),
                        pltpu.VMEM((2, n_streams, ATT_TK, qb), F32)],
        compiler_params=_params("arbitrary", "arbitrary", "arbitrary"),
        name="mla_attention",
    )(q, k, vt, g)


def _conv_compute(xb, w_ref, cw_ref, cb_ref, lg_ref, lb_ref, hbuf, y_s, *, ts, width, row_block):
    W = width
    s_idx = pl.program_id(1)

    @pl.when(s_idx == 0)
    def _():
        hbuf[0, 0:CONV_HALO, :] = jnp.zeros((CONV_HALO, W), F32)

    @pl.when(s_idx > 0)
    def _():
        hbuf[0, 0:CONV_HALO, :] = hbuf[0, ts:ts + CONV_HALO, :]

    h = jnp.dot(xb, w_ref[...], preferred_element_type=F32)
    hbuf[0, CONV_HALO:CONV_HALO + ts, :] = h[:, :W] * _sigmoid(h[:, W:2 * W])
    shifted_rows = ts + CONV_HALO - SUBLANE
    for r in range(1, SUBLANE):
        hbuf[r, 0:shifted_rows, :] = hbuf[0, r:r + shifted_rows, :]
    first_tap = CONV_HALO - (CONV_KERNEL - 1)
    for rb in range(ts // row_block):
        r0 = rb * row_block
        acc = jnp.zeros((row_block, W), F32) + cb_ref[...]
        for j in range(CONV_KERNEL):
            q8, r = divmod(first_tap + j, SUBLANE)
            acc = acc + cw_ref[j:j + 1, :] * hbuf[r, r0 + q8 * SUBLANE:r0 + q8 * SUBLANE + row_block, :]
        mu = jnp.mean(acc, axis=-1, keepdims=True)
        d = acc - mu
        var = jnp.mean(d * d, axis=-1, keepdims=True)
        y = _silu(d * lax.rsqrt(var + LN_EPS) * lg_ref[...] + lb_ref[...])
        y_s[r0:r0 + row_block, :] = (y * _silu(h[r0:r0 + row_block, 2 * W:])).astype(y_s.dtype)


def _xattn_compute(xb, mem_ref, w_ref, wkv_ref, kv_s, y_s, *, width, scale):
    W = width

    @pl.when(pl.program_id(1) == 0)
    def _():
        kv_s[...] = jnp.dot(mem_ref[0].astype(BF16), wkv_ref[...], preferred_element_type=F32).astype(BF16)

    h = jnp.dot(xb, w_ref[...], preferred_element_type=F32)
    for i in range(XATTN_HEADS):
        ls = slice(i * LANE, (i + 1) * LANE)
        s = _dot_nt(h[:, ls] * scale, kv_s[:, ls])
        p = jnp.exp(s - jnp.max(s, axis=-1, keepdims=True))
        o = _dot(p, kv_s[:, W + i * LANE:W + (i + 1) * LANE]) / jnp.sum(p, axis=-1, keepdims=True)
        y_s[:, ls] = (o * _silu(h[:, W + i * LANE:W + (i + 1) * LANE])).astype(y_s.dtype)


def _tail_kernel(x_ref, y0_ref, y1_ref, mem_ref, wc_ref, cw_ref, cb_ref, clg_ref, clb_ref, wx_ref, wkv_ref,
                 wm_ref, bg_ref, wo_ref, wout_ref, lg_ref, lb_ref, o_ref, hbuf, kv_s, yc_s, ym_s,
                 *, alpha, ts, width):
    x = x_ref[0]
    xb = x.astype(BF16)
    D = x.shape[-1]
    _conv_compute(xb, wc_ref, cw_ref, cb_ref, clg_ref, clb_ref, hbuf, yc_s, ts=ts, width=width, row_block=32)
    _xattn_compute(xb, mem_ref, wx_ref, wkv_ref, kv_s, ym_s, width=width,
                   scale=float(width // XATTN_HEADS) ** -0.5)
    merged = None
    for n, y in enumerate((y0_ref[0], y1_ref[0], yc_s[...], ym_s[...])):
        gate = _sigmoid(jnp.dot(xb, wm_ref[:, n * D:(n + 1) * D], preferred_element_type=F32) + bg_ref[n:n + 1, :])
        term = gate * jnp.dot(y, wo_ref[n], preferred_element_type=F32)
        merged = term if merged is None else merged + term
    z = alpha * x + _dot(merged, wout_ref[...])
    mu = jnp.mean(z, axis=-1, keepdims=True)
    d = z - mu
    var = jnp.mean(d * d, axis=-1, keepdims=True)
    o_ref[0] = d * lax.rsqrt(var + LN_EPS) * lg_ref[...] + lb_ref[...]


def _tail(x, y_rwkv, y_mla, mem, w_conv, conv_w, conv_b, conv_ln_g, conv_ln_b, w_x, w_mem_kv,
          w_merge, b_gate, w_o, w_out, ln_g, ln_b, alpha, ts):
    B, S, D = x.shape
    M = mem.shape[1]
    W = y_rwkv.shape[-1]
    consts = (w_conv, conv_w, conv_b, conv_ln_g, conv_ln_b, w_x, w_mem_kv, w_merge, b_gate, w_o, w_out, ln_g, ln_b)
    tile = lambda n: pl.BlockSpec((1, ts, n), lambda b, s: (b, s, 0))
    return pl.pallas_call(
        functools.partial(_tail_kernel, alpha=alpha, ts=ts, width=W),
        grid=(B, S // ts),
        in_specs=[tile(D), tile(W), tile(W), pl.BlockSpec((1, M, D), lambda b, s: (b, 0, 0))]
        + [_const_spec(a.shape, single_buffer=True) for a in consts],
        out_specs=tile(D),
        out_shape=jax.ShapeDtypeStruct((B, S, D), F32),
        scratch_shapes=[pltpu.VMEM((SUBLANE, ts + CONV_HALO, W), F32), pltpu.VMEM((M, 2 * W), BF16),
                        pltpu.VMEM((ts, W), BF16), pltpu.VMEM((ts, W), BF16)],
        compiler_params=_params("arbitrary", "arbitrary"),
        name="conv_xattn_merge",
    )(x, y_rwkv, y_mla, mem, *consts)


def _rotate_half_cols(w):
    half = w.shape[-1] // 2
    return jnp.concatenate([-w[..., half:], w[..., :half]], axis=-1)


def kernel(x, mem, positions, w_in, b_gate, rwkv_mu, rwkv_w0, rwkv_w2, rwkv_a0, rwkv_a2, rwkv_k_k, rwkv_k_a, rwkv_r_k, rwkv_lnx_g, rwkv_lnx_b, mla_q_norm, mla_w_uq, mla_kv_norm, mla_w_ukv, conv_w, conv_b, conv_ln_g, conv_ln_b, xattn_w_mem_kv, w_o_branch, w_out, ln_g, ln_b):
    B, S, D = x.shape
    depth = w_in.shape[0]
    W = D // 2
    q_lora, kv_lora = mla_q_norm.shape[-1], mla_kv_norm.shape[-1]
    heads = W // MLA_V
    P = 3 * W + 2 * RWKV_LORA
    alpha = (2.0 * depth) ** 0.25
    assert W % LANE == 0 and LANE == MLA_NOPE + MLA_V and MLA_NOPE + MLA_ROPE <= LANE
    ts = min(S, 256)
    qb = min(S, 256)
    tq = min(S, 2 * qb)
    assert S % ts == 0 and ts % CHUNK == 0 and ts % ATT_TK == 0 and S % tq == 0 and tq % qb == 0 and qb % ATT_TK == 0

    sizes = (P, W, q_lora, kv_lora, MLA_ROPE, W, 2 * W, W, W, W, N_BRANCH * D)
    offs = [0]
    for n in sizes:
        offs.append(offs[-1] + n)
    o_rw, _, o_q, o_kv, o_kpe, o_mg, o_cu, _, o_xq, _, o_merge, o_end = offs
    assert o_end == w_in.shape[-1]

    inv_freq = ROPE_THETA ** (-jnp.arange(0, MLA_ROPE, 2, dtype=F32) / MLA_ROPE)
    zeros_f = lambda n: jnp.zeros((n,), F32)
    freq128 = jnp.concatenate([zeros_f(MLA_NOPE), inv_freq, inv_freq, zeros_f(LANE - MLA_NOPE - MLA_ROPE)])[None, :]
    cos, sin = _rope_tables(positions, freq128)

    lane_head = jnp.arange(MXU_WIDTH) // RWKV_HEAD_DIM
    seg = (lane_head[:, None] == lane_head[None, :]).astype(BF16)
    zl = jnp.zeros((RWKV_LORA, W), F32)
    row = lambda a: a.reshape(1, -1)

    for l in range(depth):
        wi = w_in[l]
        w_rw = wi[:, o_rw:o_q].astype(BF16)
        w2p = jnp.concatenate([rwkv_w2[l], zl], axis=0).astype(BF16)
        a2p = jnp.concatenate([zl, rwkv_a2[l]], axis=0).astype(BF16)
        y_rwkv = _rwkv_branch(x, w_rw, row(rwkv_mu[l]), row(rwkv_w0[l]), w2p, row(rwkv_a0[l]), a2p,
                              row(rwkv_k_k[l]), row(rwkv_k_a[l]), row(rwkv_r_k[l]), row(rwkv_lnx_g[l]),
                              row(rwkv_lnx_b[l]), seg, ts)

        kpe_w = wi[:, o_kpe:o_mg]
        pad_l, pad_r = jnp.zeros((D, MLA_NOPE), F32), jnp.zeros((D, LANE - MLA_NOPE - MLA_ROPE), F32)
        w_mla = jnp.concatenate([wi[:, o_q:o_kpe], pad_l, kpe_w, pad_r, pad_l, _rotate_half_cols(kpe_w), pad_r,
                                 wi[:, o_mg:o_cu]], axis=1).astype(BF16)
        wq = mla_w_uq[l].reshape(q_lora, heads, MLA_NOPE + MLA_ROPE)
        q_nope, q_pe = wq[..., :MLA_NOPE], wq[..., MLA_NOPE:]
        zq = lambda n: jnp.zeros((q_lora, heads, n), F32)
        wq_plain = jnp.concatenate([q_nope, q_pe, zq(LANE - MLA_NOPE - MLA_ROPE)], axis=-1)
        wq_rot = jnp.concatenate([zq(MLA_NOPE), _rotate_half_cols(q_pe), zq(LANE - MLA_NOPE - MLA_ROPE)], axis=-1)
        wq2 = jnp.concatenate([wq_plain.reshape(q_lora, -1), wq_rot.reshape(q_lora, -1)], axis=1).astype(BF16)
        wkv = mla_w_ukv[l].reshape(kv_lora, heads, MLA_NOPE + MLA_V)
        wk = jnp.concatenate([wkv[..., :MLA_NOPE], jnp.zeros((kv_lora, heads, LANE - MLA_NOPE), F32)], axis=-1)
        wv = jnp.concatenate([wkv[..., MLA_NOPE:], jnp.zeros((kv_lora, heads, LANE - MLA_V), F32)], axis=-1)
        wkv2 = jnp.concatenate([wk.reshape(kv_lora, -1), wv.reshape(kv_lora, -1)], axis=1).astype(BF16)
        q, k, vt, g = _mla_prep(x, w_mla, row(mla_q_norm[l]), row(mla_kv_norm[l]), wq2, wkv2, cos, sin, heads, ts)
        y_mla = _mla_attention(q, k, vt, g, tq, qb)

        x = _tail(x, y_rwkv, y_mla, mem, wi[:, o_cu:o_xq].astype(BF16), conv_w[l], row(conv_b[l]),
                  row(conv_ln_g[l]), row(conv_ln_b[l]), wi[:, o_xq:o_merge].astype(BF16),
                  xattn_w_mem_kv[l].astype(BF16), wi[:, o_merge:].astype(BF16), b_gate[l],
                  w_o_branch[l].astype(BF16), w_out[l].astype(BF16), row(ln_g[l]), row(ln_b[l]), alpha, ts)
    return x
```

```python
import functools

import jax
import jax.numpy as jnp
from jax import lax
from jax.experimental import pallas as pl
from jax.experimental.pallas import tpu as pltpu

F32 = jnp.float32
BF16 = jnp.bfloat16

N_BRANCH = 4
RWKV_HEAD_DIM = 64
RWKV_LORA = 64
RWKV_GN_EPS = 64e-5
MLA_NOPE = 64
MLA_ROPE = 32
MLA_V = 64
ROPE_THETA = 10000.0
CONV_KERNEL = 31
XATTN_HEADS = 4
LN_EPS = 1e-5
RMS_EPS = 1e-6

LANE = 128
SUBLANE = 8
MXU_WIDTH = 256
CHUNK = 64
CONV_HALO = 32
ATT_TK = 128
ATT_VROWS = 80
LOG2_E = 1.4426950408889634
VMEM_LIMIT = 56 * 1024 * 1024


def _dot(a, b):
    return jnp.dot(a.astype(BF16), b.astype(BF16), preferred_element_type=F32)


def _dot_nt(a, b):
    return lax.dot_general(a.astype(BF16), b.astype(BF16), (((1,), (1,)), ((), ())),
                           preferred_element_type=F32)


def _split_hi_lo(a):
    hi = a.astype(BF16)
    lo = (a - hi.astype(F32)).astype(BF16)
    return hi, lo


def _dot_0_1_rhs(a, ones_like_matrix):
    hi, lo = _split_hi_lo(a)
    return (jnp.dot(hi, ones_like_matrix, preferred_element_type=F32)
            + jnp.dot(lo, ones_like_matrix, preferred_element_type=F32))


def _head_sums(a, seg):
    g = seg.shape[0]
    return jnp.concatenate([_dot_0_1_rhs(a[:, o:o + g], seg) for o in range(0, a.shape[1], g)], axis=1)


def _sigmoid(x):
    return 1.0 / (1.0 + jnp.exp(-x))


def _silu(x):
    return x * _sigmoid(x)


def _const_spec(shape, single_buffer=False):
    nd = len(shape)
    mode = pl.Buffered(1) if single_buffer else None
    return pl.BlockSpec(shape, lambda *_: (0,) * nd, pipeline_mode=mode)


def _params(*sem):
    return pltpu.CompilerParams(dimension_semantics=sem, vmem_limit_bytes=VMEM_LIMIT)


def _rope_kernel(pos_ref, f_ref, cos_ref, sin_ref):
    ang = pos_ref[0].astype(F32) * f_ref[...]
    cos_ref[0] = jnp.cos(ang)
    sin_ref[0] = jnp.sin(ang)


def _rope_tables(positions, freq128):
    B, S = positions.shape
    return pl.pallas_call(
        _rope_kernel,
        grid=(B,),
        in_specs=[pl.BlockSpec((1, S, 1), lambda b: (b, 0, 0)), _const_spec((1, LANE))],
        out_specs=[pl.BlockSpec((1, S, LANE), lambda b: (b, 0, 0))] * 2,
        out_shape=[jax.ShapeDtypeStruct((B, S, LANE), F32)] * 2,
        compiler_params=_params("arbitrary"),
        name="rope_tables",
    )(positions[..., None], freq128)


def _rwkv_kernel(x_ref, w_ref, mu_ref, w0_ref, w2_ref, a0_ref, a2_ref, kk_ref, ka_ref, rk_ref,
                 lng_ref, lnb_ref, seg_ref, o_ref,
                 carry_ref, z_ref, r_s, k_s, v_s, g_s, at_s, rt_s, bt_s, kt_s, bh_s, kh_s, et_s, y_s, *, ts, width):
    W = width
    P = 3 * W + 2 * RWKV_LORA
    n_pairs = W // LANE
    s_idx = pl.program_id(1)

    @pl.when(s_idx == 0)
    def _():
        carry_ref[...] = jnp.zeros_like(carry_ref)
        z_ref[...] = jnp.zeros_like(z_ref)

    h = jnp.dot(x_ref[0].astype(BF16), w_ref[...], preferred_element_type=F32)
    hp = h[:, :P]
    rows = lax.broadcasted_iota(jnp.int32, hp.shape, 0)
    prev = jnp.where(rows == 0, carry_ref[0:1, :], pltpu.roll(hp, 1, 0))
    carry_ref[0:1, :] = hp[ts - 1:ts, :]
    p = hp + mu_ref[...] * (prev - hp)
    r, k, v, wa = p[:, :W], p[:, W:2 * W], p[:, 2 * W:3 * W], p[:, 3 * W:]

    zz = -(w0_ref[...] + _dot(jnp.tanh(wa), w2_ref[...]))
    softplus = jnp.maximum(zz, 0.0) + jnp.log(1.0 + jnp.exp(-jnp.abs(zz)))
    lw = -jnp.exp(-softplus - 0.5)
    a = _sigmoid(a0_ref[...] + _dot(wa, a2_ref[...]))
    kk = k * kk_ref[...]
    ss = _head_sums(kk * kk, seg_ref[...])
    kkn = kk * lax.rsqrt(jnp.maximum(ss, 1e-24))
    kmod = k * (1.0 + (a - 1.0) * ka_ref[...])
    b = kkn * a

    C = CHUNK
    n_chunks = ts // C
    rr = lax.broadcasted_iota(jnp.int32, (ts, ts), 0)
    cc = lax.broadcasted_iota(jnp.int32, (ts, ts), 1)
    same_chunk = (rr // C) == (cc // C)
    lw_hi, lw_lo = _split_hi_lo(lw)
    tri = jnp.where(same_chunk & (rr >= cc), 1.0, 0.0).astype(BF16)
    blk = jnp.where(same_chunk, 1.0, 0.0).astype(BF16)
    cs = jnp.dot(tri, lw_hi, preferred_element_type=F32) + jnp.dot(tri, lw_lo, preferred_element_type=F32)
    tot = jnp.dot(blk, lw_hi, preferred_element_type=F32) + jnp.dot(blk, lw_lo, preferred_element_type=F32)
    w_inv, w_end = jnp.exp(-cs), jnp.exp(tot - cs)
    r_s[...] = r
    k_s[...] = kmod
    v_s[...] = v
    g_s[...] = _silu(h[:, P:])
    at_s[...] = -kkn * jnp.exp(cs - lw)
    rt_s[...] = r * jnp.exp(cs)
    bt_s[...] = b * w_inv
    kt_s[...] = kmod * w_inv
    bh_s[...] = b * w_end
    kh_s[...] = kmod * w_end
    et_s[...] = jnp.exp(tot)

    units = [(c, pr) for c in range(n_chunks) for pr in range(n_pairs)]
    first_head = lax.broadcasted_iota(jnp.int32, (C, LANE), 1) < RWKV_HEAD_DIM

    def stacked(ref):
        out = []
        for c, pr in units:
            t = ref[c * C:(c + 1) * C, pr * LANE:(pr + 1) * LANE]
            out.append(jnp.concatenate([jnp.where(first_head, t, 0.0), jnp.where(first_head, 0.0, t)],
                                       axis=0).astype(BF16))
        return jnp.stack(out)

    def bmm(x, y):
        return jnp.einsum('umk,ukn->umn', x.astype(BF16), y.astype(BF16), preferred_element_type=F32)

    def bmm_nt(x, y):
        return jnp.einsum('umk,unk->umn', x.astype(BF16), y.astype(BF16), preferred_element_type=F32)

    def bmm_tn(x, y):
        return jnp.einsum('ukm,ukn->umn', x.astype(BF16), y.astype(BF16), preferred_element_type=F32)

    ri = lax.broadcasted_iota(jnp.int32, (2 * C, 2 * C), 0)
    ci = lax.broadcasted_iota(jnp.int32, (2 * C, 2 * C), 1)
    strict = (ri > ci)[None]
    incl = (ri >= ci)[None]
    eye = (ri == ci).astype(F32)[None]

    As, Rs, Bs, Ks = stacked(at_s), stacked(rt_s), stacked(bt_s), stacked(kt_s)
    Vs, Bh, Kh = stacked(v_s), stacked(bh_s), stacked(kh_s)
    G = bmm_nt(jnp.concatenate([As, Rs], axis=1), jnp.concatenate([Bs, Ks], axis=1))
    L = jnp.where(strict, G[:, :2 * C, :2 * C], 0.0)
    Lak = jnp.where(strict, G[:, :2 * C, 2 * C:], 0.0)
    Mrb = jnp.where(incl, G[:, 2 * C:, :2 * C], 0.0)
    Mrk = jnp.where(incl, G[:, 2 * C:, 2 * C:], 0.0)
    T = eye + L
    Lb = L.astype(BF16)
    Pw = bmm(Lb, Lb)
    step = 2
    while step * 2 < C:
        Pb = Pw.astype(BF16)
        R2 = bmm(Pb, jnp.concatenate([Pb, T.astype(BF16)], axis=2))
        Pw, T = R2[:, :, :2 * C], T + R2[:, :, 2 * C:]
        step *= 2
    T = T + bmm(Pw, T)
    X = bmm(Lak, Vs)
    AU = bmm(T, jnp.concatenate([As, X.astype(BF16)], axis=2))
    Ah, U0 = AU[:, :, :LANE], AU[:, :, LANE:]
    Gc = bmm_tn(Ah, Bh)
    Hc = bmm_tn(jnp.concatenate([U0.astype(BF16), Vs], axis=1), jnp.concatenate([Bh, Kh], axis=1))
    Z = z_ref[...]
    starts = []
    for c in range(n_chunks):
        us = slice(c * n_pairs, (c + 1) * n_pairs)
        e_tot = jnp.stack([et_s[c * C:c * C + 1, pr * LANE:(pr + 1) * LANE] for pr in range(n_pairs)])
        starts.append(Z)
        Z = Z * e_tot + bmm(Z, Gc[us]) + Hc[us]
    z_ref[...] = Z
    Z0 = jnp.concatenate(starts, axis=0)
    AR = bmm_nt(jnp.concatenate([Ah.astype(BF16), Rs], axis=1), Z0)
    Uu = AR[:, :2 * C] + U0
    Ys = AR[:, 2 * C:] + bmm(jnp.concatenate([Mrb.astype(BF16), Mrk.astype(BF16)], axis=2),
                             jnp.concatenate([Uu.astype(BF16), Vs], axis=1))
    ysum = Ys[:, :C] + Ys[:, C:]
    for u, (c, pr) in enumerate(units):
        y_s[c * C:(c + 1) * C, pr * LANE:(pr + 1) * LANE] = ysum[u]

    y = y_s[...]
    inv_n = 1.0 / RWKV_HEAD_DIM
    y_mu = _head_sums(y, seg_ref[...]) * inv_n
    d = y - y_mu
    y_var = _head_sums(d * d, seg_ref[...]) * inv_n
    yn = d * lax.rsqrt(y_var + RWKV_GN_EPS) * lng_ref[...] + lnb_ref[...]
    bonus = _head_sums(r_s[...] * k_s[...] * rk_ref[...], seg_ref[...]) * v_s[...]
    o_ref[0] = ((yn + bonus) * g_s[...]).astype(o_ref.dtype)


def _rwkv_branch(x, w_rw, mu, w0, w2p, a0, a2p, k_k, k_a, r_k, lnx_g, lnx_b, seg, ts):
    B, S, D = x.shape
    W = w0.shape[-1]
    P = 3 * W + 2 * RWKV_LORA
    n_pairs = W // LANE
    row = lambda n: _const_spec((1, n))
    scr = lambda: pltpu.VMEM((ts, W), F32)
    return pl.pallas_call(
        functools.partial(_rwkv_kernel, ts=ts, width=W),
        grid=(B, S // ts),
        in_specs=[pl.BlockSpec((1, ts, D), lambda b, s: (b, s, 0)),
                  _const_spec(w_rw.shape), row(P), row(W), _const_spec(w2p.shape), row(W),
                  _const_spec(a2p.shape), row(W), row(W), row(W), row(W), row(W), _const_spec(seg.shape)],
        out_specs=pl.BlockSpec((1, ts, W), lambda b, s: (b, s, 0)),
        out_shape=jax.ShapeDtypeStruct((B, S, W), BF16),
        scratch_shapes=[pltpu.VMEM((8, P), F32), pltpu.VMEM((n_pairs, LANE, LANE), F32),
                        scr(), scr(), scr(), scr(), scr(), scr(), scr(), scr(), scr(), scr(), scr(), scr()],
        compiler_params=_params("arbitrary", "arbitrary"),
        name="rwkv_branch",
    )(x, w_rw, mu, w0, w2p, a0, a2p, k_k, k_a, r_k, lnx_g, lnx_b, seg)


def _mla_prep_kernel(x_ref, w_ref, qn_ref, kvn_ref, wq_ref, wkv_ref, cos_ref, sin_ref,
                     q_ref, k_ref, vt_ref, g_ref, *, q_lora, kv_lora, heads, scale):
    h = jnp.dot(x_ref[0].astype(BF16), w_ref[...], preferred_element_type=F32)
    ql = h[:, :q_lora]
    kvl = h[:, q_lora:q_lora + kv_lora]
    o = q_lora + kv_lora
    kpe, kpe_rot = h[:, o:o + LANE], h[:, o + LANE:o + 2 * LANE]
    gate = h[:, o + 2 * LANE:]
    cos, sin = cos_ref[0], sin_ref[0]
    qn = ql * lax.rsqrt(jnp.mean(ql * ql, axis=-1, keepdims=True) + RMS_EPS) * qn_ref[...]
    kvn = kvl * lax.rsqrt(jnp.mean(kvl * kvl, axis=-1, keepdims=True) + RMS_EPS) * kvn_ref[...]
    q2 = _dot(qn, wq_ref[...])
    kv = _dot(kvn, wkv_ref[...])
    hw = heads * LANE
    k_pe = kpe * cos + kpe_rot * sin
    ones_lane = jnp.where(lax.broadcasted_iota(jnp.int32, (1, LANE), 1) == MLA_V, 1.0, 0.0)
    ts = h.shape[0]
    for i in range(heads):
        ls = slice(i * LANE, (i + 1) * LANE)
        q_ref[0, :, ls] = ((q2[:, ls] * cos + q2[:, hw + i * LANE:hw + (i + 1) * LANE] * sin) * scale).astype(BF16)
        k_ref[0, :, ls] = (kv[:, ls] + k_pe).astype(BF16)
        v_t = (kv[:, hw + i * LANE:hw + (i + 1) * LANE] + ones_lane).T
        for t in range(ts // ATT_TK):
            vt_ref[0, i, t] = v_t[:ATT_VROWS, t * ATT_TK:(t + 1) * ATT_TK].astype(BF16)
    g_ref[0] = _silu(gate).astype(BF16)


def _mla_prep(x, w_mla, q_norm, kv_norm, wq2, wkv2, cos, sin, heads, ts):
    B, S, D = x.shape
    q_lora, kv_lora = q_norm.shape[-1], kv_norm.shape[-1]
    W = heads * MLA_V
    tile = lambda n: pl.BlockSpec((1, ts, n), lambda b, s: (b, s, 0))
    return pl.pallas_call(
        functools.partial(_mla_prep_kernel, q_lora=q_lora, kv_lora=kv_lora, heads=heads,
                          scale=float(MLA_NOPE + MLA_ROPE) ** -0.5 * LOG2_E),
        grid=(B, S // ts),
        in_specs=[tile(D), _const_spec(w_mla.shape), _const_spec((1, q_lora)), _const_spec((1, kv_lora)),
                  _const_spec(wq2.shape), _const_spec(wkv2.shape), tile(LANE), tile(LANE)],
        out_specs=[tile(heads * LANE), tile(heads * LANE),
                   pl.BlockSpec((1, heads, ts // ATT_TK, ATT_VROWS, ATT_TK), lambda b, s: (b, 0, s, 0, 0)), tile(W)],
        out_shape=[jax.ShapeDtypeStruct((B, S, heads * LANE), BF16)] * 2
        + [jax.ShapeDtypeStruct((B, heads, S // ATT_TK, ATT_VROWS, ATT_TK), BF16), jax.ShapeDtypeStruct((B, S, W), BF16)],
        compiler_params=_params("arbitrary", "arbitrary"),
        name="mla_prep",
    )(x, w_mla, q_norm, kv_norm, wq2, wkv2, cos, sin)


def _mla_attn_kernel(q_ref, k_ref, vt_ref, g_ref, o_ref, m_s, acc_s, s_s, *, tq, qb):
    i = pl.program_id(2)
    tk = ATT_TK
    n_blk = tq // qb
    streams = [(e, hb) for e in range(2) for hb in range(n_blk)]
    key_i = lax.broadcasted_iota(jnp.int32, (tk, qb), 0)
    qry_i = lax.broadcasted_iota(jnp.int32, (tk, qb), 1)

    def scores(j, mode, buf):
        rows_k = pl.ds(pl.multiple_of(j * tk, tk), tk)
        for idx, (e, hb) in enumerate(streams):
            if mode[hb] is not None:
                ls = slice(e * LANE, (e + 1) * LANE)
                s_s[buf, idx] = lax.dot_general(k_ref[0, rows_k, ls], q_ref[0, hb * qb:(hb + 1) * qb, ls],
                                                (((1,), (1,)), ((), ())), preferred_element_type=F32)

    def consume(j, mode, buf):
        for idx, (e, hb) in enumerate(streams):
            if mode[hb] is None:
                continue
            s = s_s[buf, idx]
            if mode[hb] is not False:
                s = jnp.where(key_i + mode[hb] <= qry_i, s, -1e30)
            m = m_s[idx, 0:1, :]
            m_new = jnp.maximum(m, jnp.max(s, axis=0, keepdims=True))
            p = jnp.exp2(s - m_new)
            acc_s[idx] = jnp.exp2(m - m_new) * acc_s[idx] + jnp.dot(vt_ref[0, e, j], p.astype(BF16),
                                                                    preferred_element_type=F32)
            m_s[idx, 0:1, :] = m_new

    tiles_per_blk = qb // tk
    n_diag = tq // tk
    assert n_diag % 2 == 0
    first_diag = i * n_diag
    visible = [False] * n_blk

    def diag_mode(d):
        mode = []
        for hb in range(n_blk):
            rel = d - hb * tiles_per_blk
            mode.append(False if rel < 0 else (None if rel >= tiles_per_blk else rel * tk))
        return mode

    m_s[...] = jnp.full(m_s.shape, -1e30, F32)
    acc_s[...] = jnp.zeros(acc_s.shape, F32)
    scores(0, visible, 0)

    def body(jj, carry):
        for u in range(n_diag):
            scores(n_diag * jj + u + 1, visible, (u + 1) % 2)
            consume(n_diag * jj + u, visible, u % 2)
        return carry

    lax.fori_loop(0, i, body, 0)
    for d in range(n_diag):
        if d + 1 < n_diag:
            scores(first_diag + d + 1, diag_mode(d + 1), (d + 1) % 2)
        consume(first_diag + d, diag_mode(d), d % 2)

    for hb in range(n_blk):
        rows = slice(hb * qb, (hb + 1) * qb)
        a0, a1 = acc_s[streams.index((0, hb))], acc_s[streams.index((1, hb))]
        o_t = jnp.concatenate([a0[:MLA_V] / a0[MLA_V:MLA_V + 1], a1[:MLA_V] / a1[MLA_V:MLA_V + 1]], axis=0)
        o_ref[0, rows, :] = (o_t.T * g_ref[0, rows, :].astype(F32)).astype(o_ref.dtype)


def _mla_attention(q, k, vt, g, tq, qb):
    B, S, HW = q.shape
    W = g.shape[-1]
    n_pairs = W // LANE
    n_streams = 2 * (tq // qb)
    return pl.pallas_call(
        functools.partial(_mla_attn_kernel, tq=tq, qb=qb),
        grid=(B, n_pairs, S // tq),
        in_specs=[pl.BlockSpec((1, tq, 2 * LANE), lambda b, p, i: (b, i, p)),
                  pl.BlockSpec((1, S, 2 * LANE), lambda b, p, i: (b, 0, p)),
                  pl.BlockSpec((1, 2, S // ATT_TK, ATT_VROWS, ATT_TK), lambda b, p, i: (b, p, 0, 0, 0)),
                  pl.BlockSpec((1, tq, LANE), lambda b, p, i: (b, i, p))],
        out_specs=pl.BlockSpec((1, tq, LANE), lambda b, p, i: (b, i, p)),
        out_shape=jax.ShapeDtypeStruct((B, S, W), BF16),
        scratch_shapes=[pltpu.VMEM((n_streams, SUBLANE, qb), F32), pltpu.VMEM((n_streams, ATT_VROWS, qb), F32),
                        pltpu.VMEM((2, n_streams, ATT_TK, qb), F32)],
        compiler_params=_params("arbitrary", "arbitrary", "arbitrary"),
        name="mla_attention",
    )(q, k, vt, g)


def _conv_compute(xb, w_ref, cw_ref, cb_ref, lg_ref, lb_ref, hbuf, y_s, *, ts, width):
    W = width
    s_idx = pl.program_id(1)

    @pl.when(s_idx == 0)
    def _():
        hbuf[0, 0:CONV_HALO, :] = jnp.zeros((CONV_HALO, W), F32)

    @pl.when(s_idx > 0)
    def _():
        hbuf[0, 0:CONV_HALO, :] = hbuf[0, ts:ts + CONV_HALO, :]

    h = jnp.dot(xb, w_ref[...], preferred_element_type=F32)
    hbuf[0, CONV_HALO:CONV_HALO + ts, :] = h[:, :W] * _sigmoid(h[:, W:2 * W])
    shifted_rows = ts + CONV_HALO - SUBLANE
    for r in range(1, SUBLANE):
        hbuf[r, 0:shifted_rows, :] = hbuf[0, r:r + shifted_rows, :]
    first_tap = CONV_HALO - (CONV_KERNEL - 1)
    acc = jnp.zeros((ts, W), F32) + cb_ref[...]
    for j in range(CONV_KERNEL):
        q8, r = divmod(first_tap + j, SUBLANE)
        acc = acc + cw_ref[j:j + 1, :] * hbuf[r, q8 * SUBLANE:q8 * SUBLANE + ts, :]
    mu = jnp.mean(acc, axis=-1, keepdims=True)
    d = acc - mu
    var = jnp.mean(d * d, axis=-1, keepdims=True)
    y = _silu(d * lax.rsqrt(var + LN_EPS) * lg_ref[...] + lb_ref[...])
    y_s[...] = (y * _silu(h[:, 2 * W:])).astype(y_s.dtype)


def _xattn_compute(xb, mem_ref, w_ref, wkv_ref, kv_s, y_s, *, width, scale):
    W = width

    @pl.when(pl.program_id(1) == 0)
    def _():
        kv_s[...] = jnp.dot(mem_ref[0].astype(BF16), wkv_ref[...], preferred_element_type=F32).astype(BF16)

    h = jnp.dot(xb, w_ref[...], preferred_element_type=F32)
    for i in range(XATTN_HEADS):
        ls = slice(i * LANE, (i + 1) * LANE)
        s = _dot_nt(h[:, ls] * scale, kv_s[:, ls])
        p = jnp.exp(s - jnp.max(s, axis=-1, keepdims=True))
        o = _dot(p, kv_s[:, W + i * LANE:W + (i + 1) * LANE]) / jnp.sum(p, axis=-1, keepdims=True)
        y_s[:, ls] = (o * _silu(h[:, W + i * LANE:W + (i + 1) * LANE])).astype(y_s.dtype)


def _tail_kernel(x_ref, y0_ref, y1_ref, mem_ref, wc_ref, cw_ref, cb_ref, clg_ref, clb_ref, wx_ref, wkv_ref,
                 wm_ref, bg_ref, wo_ref, wout_ref, lg_ref, lb_ref, o_ref, hbuf, kv_s, yc_s, ym_s,
                 *, alpha, ts, width):
    x = x_ref[0]
    xb = x.astype(BF16)
    D = x.shape[-1]
    _conv_compute(xb, wc_ref, cw_ref, cb_ref, clg_ref, clb_ref, hbuf, yc_s, ts=ts, width=width)
    _xattn_compute(xb, mem_ref, wx_ref, wkv_ref, kv_s, ym_s, width=width,
                   scale=float(width // XATTN_HEADS) ** -0.5)
    merged = None
    for n, y in enumerate((y0_ref[0], y1_ref[0], yc_s[...], ym_s[...])):
        gate = _sigmoid(jnp.dot(xb, wm_ref[:, n * D:(n + 1) * D], preferred_element_type=F32) + bg_ref[n:n + 1, :])
        term = gate * jnp.dot(y, wo_ref[n], preferred_element_type=F32)
        merged = term if merged is None else merged + term
    z = alpha * x + _dot(merged, wout_ref[...])
    mu = jnp.mean(z, axis=-1, keepdims=True)
    d = z - mu
    var = jnp.mean(d * d, axis=-1, keepdims=True)
    o_ref[0] = d * lax.rsqrt(var + LN_EPS) * lg_ref[...] + lb_ref[...]


def _tail(x, y_rwkv, y_mla, mem, w_conv, conv_w, conv_b, conv_ln_g, conv_ln_b, w_x, w_mem_kv,
          w_merge, b_gate, w_o, w_out, ln_g, ln_b, alpha, ts):
    B, S, D = x.shape
    M = mem.shape[1]
    W = y_rwkv.shape[-1]
    consts = (w_conv, conv_w, conv_b, conv_ln_g, conv_ln_b, w_x, w_mem_kv, w_merge, b_gate, w_o, w_out, ln_g, ln_b)
    tile = lambda n: pl.BlockSpec((1, ts, n), lambda b, s: (b, s, 0))
    return pl.pallas_call(
        functools.partial(_tail_kernel, alpha=alpha, ts=ts, width=W),
        grid=(B, S // ts),
        in_specs=[tile(D), tile(W), tile(W), pl.BlockSpec((1, M, D), lambda b, s: (b, 0, 0))]
        + [_const_spec(a.shape, single_buffer=True) for a in consts],
        out_specs=tile(D),
        out_shape=jax.ShapeDtypeStruct((B, S, D), F32),
        scratch_shapes=[pltpu.VMEM((SUBLANE, ts + CONV_HALO, W), F32), pltpu.VMEM((M, 2 * W), BF16),
                        pltpu.VMEM((ts, W), BF16), pltpu.VMEM((ts, W), BF16)],
        compiler_params=_params("arbitrary", "arbitrary"),
        name="conv_xattn_merge",
    )(x, y_rwkv, y_mla, mem, *consts)


def _rotate_half_cols(w):
    half = w.shape[-1] // 2
    return jnp.concatenate([-w[..., half:], w[..., :half]], axis=-1)


def kernel(x, mem, positions, w_in, b_gate, rwkv_mu, rwkv_w0, rwkv_w2, rwkv_a0, rwkv_a2, rwkv_k_k, rwkv_k_a, rwkv_r_k, rwkv_lnx_g, rwkv_lnx_b, mla_q_norm, mla_w_uq, mla_kv_norm, mla_w_ukv, conv_w, conv_b, conv_ln_g, conv_ln_b, xattn_w_mem_kv, w_o_branch, w_out, ln_g, ln_b):
    B, S, D = x.shape
    depth = w_in.shape[0]
    W = D // 2
    q_lora, kv_lora = mla_q_norm.shape[-1], mla_kv_norm.shape[-1]
    heads = W // MLA_V
    P = 3 * W + 2 * RWKV_LORA
    alpha = (2.0 * depth) ** 0.25
    assert W % LANE == 0 and LANE == MLA_NOPE + MLA_V and MLA_NOPE + MLA_ROPE <= LANE
    ts = min(S, 256)
    qb = min(S, 256)
    tq = min(S, 4 * qb)
    assert S % ts == 0 and ts % CHUNK == 0 and ts % ATT_TK == 0 and S % tq == 0 and tq % qb == 0 and qb % ATT_TK == 0

    sizes = (P, W, q_lora, kv_lora, MLA_ROPE, W, 2 * W, W, W, W, N_BRANCH * D)
    offs = [0]
    for n in sizes:
        offs.append(offs[-1] + n)
    o_rw, _, o_q, o_kv, o_kpe, o_mg, o_cu, _, o_xq, _, o_merge, o_end = offs
    assert o_end == w_in.shape[-1]

    inv_freq = ROPE_THETA ** (-jnp.arange(0, MLA_ROPE, 2, dtype=F32) / MLA_ROPE)
    zeros_f = lambda n: jnp.zeros((n,), F32)
    freq128 = jnp.concatenate([zeros_f(MLA_NOPE), inv_freq, inv_freq, zeros_f(LANE - MLA_NOPE - MLA_ROPE)])[None, :]
    cos, sin = _rope_tables(positions, freq128)

    lane_head = jnp.arange(MXU_WIDTH) // RWKV_HEAD_DIM
    seg = (lane_head[:, None] == lane_head[None, :]).astype(BF16)
    zl = jnp.zeros((RWKV_LORA, W), F32)
    row = lambda a: a.reshape(1, -1)

    for l in range(depth):
        wi = w_in[l]
        w_rw = wi[:, o_rw:o_q].astype(BF16)
        w2p = jnp.concatenate([rwkv_w2[l], zl], axis=0).astype(BF16)
        a2p = jnp.concatenate([zl, rwkv_a2[l]], axis=0).astype(BF16)
        y_rwkv = _rwkv_branch(x, w_rw, row(rwkv_mu[l]), row(rwkv_w0[l]), w2p, row(rwkv_a0[l]), a2p,
                              row(rwkv_k_k[l]), row(rwkv_k_a[l]), row(rwkv_r_k[l]), row(rwkv_lnx_g[l]),
                              row(rwkv_lnx_b[l]), seg, ts)

        kpe_w = wi[:, o_kpe:o_mg]
        pad_l, pad_r = jnp.zeros((D, MLA_NOPE), F32), jnp.zeros((D, LANE - MLA_NOPE - MLA_ROPE), F32)
        w_mla = jnp.concatenate([wi[:, o_q:o_kpe], pad_l, kpe_w, pad_r, pad_l, _rotate_half_cols(kpe_w), pad_r,
                                 wi[:, o_mg:o_cu]], axis=1).astype(BF16)
        wq = mla_w_uq[l].reshape(q_lora, heads, MLA_NOPE + MLA_ROPE)
        q_nope, q_pe = wq[..., :MLA_NOPE], wq[..., MLA_NOPE:]
        zq = lambda n: jnp.zeros((q_lora, heads, n), F32)
        wq_plain = jnp.concatenate([q_nope, q_pe, zq(LANE - MLA_NOPE - MLA_ROPE)], axis=-1)
        wq_rot = jnp.concatenate([zq(MLA_NOPE), _rotate_half_cols(q_pe), zq(LANE - MLA_NOPE - MLA_ROPE)], axis=-1)
        wq2 = jnp.concatenate([wq_plain.reshape(q_lora, -1), wq_rot.reshape(q_lora, -1)], axis=1).astype(BF16)
        wkv = mla_w_ukv[l].reshape(kv_lora, heads, MLA_NOPE + MLA_V)
        wk = jnp.concatenate([wkv[..., :MLA_NOPE], jnp.zeros((kv_lora, heads, LANE - MLA_NOPE), F32)], axis=-1)
        wv = jnp.concatenate([wkv[..., MLA_NOPE:], jnp.zeros((kv_lora, heads, LANE - MLA_V), F32)], axis=-1)
        wkv2 = jnp.concatenate([wk.reshape(kv_lora, -1), wv.reshape(kv_lora, -1)], axis=1).astype(BF16)
        q, k, vt, g = _mla_prep(x, w_mla, row(mla_q_norm[l]), row(mla_kv_norm[l]), wq2, wkv2, cos, sin, heads, ts)
        y_mla = _mla_attention(q, k, vt, g, tq, qb)

        x = _tail(x, y_rwkv, y_mla, mem, wi[:, o_cu:o_xq].astype(BF16), conv_w[l], row(conv_b[l]),
                  row(conv_ln_g[l]), row(conv_ln_b[l]), wi[:, o_xq:o_merge].astype(BF16),
                  xattn_w_mem_kv[l].astype(BF16), wi[:, o_merge:].astype(BF16), b_gate[l],
                  w_o_branch[l].astype(BF16), w_out[l].astype(BF16), row(ln_g[l]), row(ln_b[l]), alpha, min(S, 2 * ts))
    return x
```

```python
import functools

import jax
import jax.numpy as jnp
from jax import lax
from jax.experimental import pallas as pl
from jax.experimental.pallas import tpu as pltpu

F32 = jnp.float32
BF16 = jnp.bfloat16

N_BRANCH = 4
RWKV_HEAD_DIM = 64
RWKV_LORA = 64
RWKV_GN_EPS = 64e-5
MLA_NOPE = 64
MLA_ROPE = 32
MLA_V = 64
ROPE_THETA = 10000.0
CONV_KERNEL = 31
XATTN_HEADS = 4
LN_EPS = 1e-5
RMS_EPS = 1e-6

LANE = 128
SUBLANE = 8
MXU_WIDTH = 256
CHUNK = 64
CONV_HALO = 32
ATT_TK = 128
ATT_VROWS = 80
LOG2_E = 1.4426950408889634
VMEM_LIMIT = 56 * 1024 * 1024


def _dot(a, b):
    return jnp.dot(a.astype(BF16), b.astype(BF16), preferred_element_type=F32)


def _dot_nt(a, b):
    return lax.dot_general(a.astype(BF16), b.astype(BF16), (((1,), (1,)), ((), ())),
                           preferred_element_type=F32)


def _split_hi_lo(a):
    hi = a.astype(BF16)
    lo = (a - hi.astype(F32)).astype(BF16)
    return hi, lo


def _dot_0_1_rhs(a, ones_like_matrix):
    hi, lo = _split_hi_lo(a)
    return (jnp.dot(hi, ones_like_matrix, preferred_element_type=F32)
            + jnp.dot(lo, ones_like_matrix, preferred_element_type=F32))


def _head_sums(a, seg):
    g = seg.shape[0]
    return jnp.concatenate([_dot_0_1_rhs(a[:, o:o + g], seg) for o in range(0, a.shape[1], g)], axis=1)


def _sigmoid(x):
    return 1.0 / (1.0 + jnp.exp(-x))


def _silu(x):
    return x * _sigmoid(x)


def _const_spec(shape, single_buffer=False):
    nd = len(shape)
    mode = pl.Buffered(1) if single_buffer else None
    return pl.BlockSpec(shape, lambda *_: (0,) * nd, pipeline_mode=mode)


def _params(*sem):
    return pltpu.CompilerParams(dimension_semantics=sem, vmem_limit_bytes=VMEM_LIMIT)


def _rope_kernel(pos_ref, f_ref, cos_ref, sin_ref):
    ang = pos_ref[0].astype(F32) * f_ref[...]
    cos_ref[0] = jnp.cos(ang)
    sin_ref[0] = jnp.sin(ang)


def _rope_tables(positions, freq128):
    B, S = positions.shape
    return pl.pallas_call(
        _rope_kernel,
        grid=(B,),
        in_specs=[pl.BlockSpec((1, S, 1), lambda b: (b, 0, 0)), _const_spec((1, LANE))],
        out_specs=[pl.BlockSpec((1, S, LANE), lambda b: (b, 0, 0))] * 2,
        out_shape=[jax.ShapeDtypeStruct((B, S, LANE), F32)] * 2,
        compiler_params=_params("arbitrary"),
        name="rope_tables",
    )(positions[..., None], freq128)


def _rwkv_kernel(x_ref, w_ref, mu_ref, w0_ref, w2_ref, a0_ref, a2_ref, kk_ref, ka_ref, rk_ref,
                 lng_ref, lnb_ref, seg_ref, o_ref,
                 carry_ref, z_ref, r_s, k_s, v_s, g_s, at_s, rt_s, bt_s, kt_s, bh_s, kh_s, et_s, y_s, *, ts, width):
    W = width
    P = 3 * W + 2 * RWKV_LORA
    n_pairs = W // LANE
    s_idx = pl.program_id(1)

    @pl.when(s_idx == 0)
    def _():
        carry_ref[...] = jnp.zeros_like(carry_ref)
        z_ref[...] = jnp.zeros_like(z_ref)

    h = jnp.dot(x_ref[0].astype(BF16), w_ref[...], preferred_element_type=F32)
    hp = h[:, :P]
    rows = lax.broadcasted_iota(jnp.int32, hp.shape, 0)
    prev = jnp.where(rows == 0, carry_ref[0:1, :], pltpu.roll(hp, 1, 0))
    carry_ref[0:1, :] = hp[ts - 1:ts, :]
    p = hp + mu_ref[...] * (prev - hp)
    r, k, v, wa = p[:, :W], p[:, W:2 * W], p[:, 2 * W:3 * W], p[:, 3 * W:]

    zz = -(w0_ref[...] + _dot(jnp.tanh(wa), w2_ref[...]))
    softplus = jnp.maximum(zz, 0.0) + jnp.log(1.0 + jnp.exp(-jnp.abs(zz)))
    lw = -jnp.exp(-softplus - 0.5)
    a = _sigmoid(a0_ref[...] + _dot(wa, a2_ref[...]))
    kk = k * kk_ref[...]
    ss = _head_sums(kk * kk, seg_ref[...])
    kkn = kk * lax.rsqrt(jnp.maximum(ss, 1e-24))
    kmod = k * (1.0 + (a - 1.0) * ka_ref[...])
    b = kkn * a

    C = CHUNK
    n_chunks = ts // C
    rr = lax.broadcasted_iota(jnp.int32, (ts, ts), 0)
    cc = lax.broadcasted_iota(jnp.int32, (ts, ts), 1)
    same_chunk = (rr // C) == (cc // C)
    lw_hi, lw_lo = _split_hi_lo(lw)
    tri = jnp.where(same_chunk & (rr >= cc), 1.0, 0.0).astype(BF16)
    blk = jnp.where(same_chunk, 1.0, 0.0).astype(BF16)
    cs = jnp.dot(tri, lw_hi, preferred_element_type=F32) + jnp.dot(tri, lw_lo, preferred_element_type=F32)
    tot = jnp.dot(blk, lw_hi, preferred_element_type=F32) + jnp.dot(blk, lw_lo, preferred_element_type=F32)
    w_inv, w_end = jnp.exp(-cs), jnp.exp(tot - cs)
    r_s[...] = r
    k_s[...] = kmod
    v_s[...] = v
    g_s[...] = _silu(h[:, P:])
    at_s[...] = -kkn * jnp.exp(cs - lw)
    rt_s[...] = r * jnp.exp(cs)
    bt_s[...] = b * w_inv
    kt_s[...] = kmod * w_inv
    bh_s[...] = b * w_end
    kh_s[...] = kmod * w_end
    et_s[...] = jnp.exp(tot)

    units = [(c, pr) for c in range(n_chunks) for pr in range(n_pairs)]
    first_head = lax.broadcasted_iota(jnp.int32, (C, LANE), 1) < RWKV_HEAD_DIM

    def stacked(ref):
        out = []
        for c, pr in units:
            t = ref[c * C:(c + 1) * C, pr * LANE:(pr + 1) * LANE]
            out.append(jnp.concatenate([jnp.where(first_head, t, 0.0), jnp.where(first_head, 0.0, t)],
                                       axis=0).astype(BF16))
        return jnp.stack(out)

    def bmm(x, y):
        return jnp.einsum('umk,ukn->umn', x.astype(BF16), y.astype(BF16), preferred_element_type=F32)

    def bmm_nt(x, y):
        return jnp.einsum('umk,unk->umn', x.astype(BF16), y.astype(BF16), preferred_element_type=F32)

    def bmm_tn(x, y):
        return jnp.einsum('ukm,ukn->umn', x.astype(BF16), y.astype(BF16), preferred_element_type=F32)

    ri = lax.broadcasted_iota(jnp.int32, (2 * C, 2 * C), 0)
    ci = lax.broadcasted_iota(jnp.int32, (2 * C, 2 * C), 1)
    strict = (ri > ci)[None]
    incl = (ri >= ci)[None]
    eye = (ri == ci).astype(F32)[None]

    As, Rs, Bs, Ks = stacked(at_s), stacked(rt_s), stacked(bt_s), stacked(kt_s)
    Vs, Bh, Kh = stacked(v_s), stacked(bh_s), stacked(kh_s)
    G = bmm_nt(jnp.concatenate([As, Rs], axis=1), jnp.concatenate([Bs, Ks], axis=1))
    L = jnp.where(strict, G[:, :2 * C, :2 * C], 0.0)
    Lak = jnp.where(strict, G[:, :2 * C, 2 * C:], 0.0)
    Mrb = jnp.where(incl, G[:, 2 * C:, :2 * C], 0.0)
    Mrk = jnp.where(incl, G[:, 2 * C:, 2 * C:], 0.0)
    T = eye + L
    Lb = L.astype(BF16)
    Pw = bmm(Lb, Lb)
    step = 2
    while step * 2 < C:
        Pb = Pw.astype(BF16)
        R2 = bmm(Pb, jnp.concatenate([Pb, T.astype(BF16)], axis=2))
        Pw, T = R2[:, :, :2 * C], T + R2[:, :, 2 * C:]
        step *= 2
    T = T + bmm(Pw, T)
    X = bmm(Lak, Vs)
    AU = bmm(T, jnp.concatenate([As, X.astype(BF16)], axis=2))
    Ah, U0 = AU[:, :, :LANE], AU[:, :, LANE:]
    Gc = bmm_tn(Ah, Bh)
    Hc = bmm_tn(jnp.concatenate([U0.astype(BF16), Vs], axis=1), jnp.concatenate([Bh, Kh], axis=1))
    Z = z_ref[...]
    starts = []
    for c in range(n_chunks):
        us = slice(c * n_pairs, (c + 1) * n_pairs)
        e_tot = jnp.stack([et_s[c * C:c * C + 1, pr * LANE:(pr + 1) * LANE] for pr in range(n_pairs)])
        starts.append(Z)
        Z = Z * e_tot + bmm(Z, Gc[us]) + Hc[us]
    z_ref[...] = Z
    Z0 = jnp.concatenate(starts, axis=0)
    AR = bmm_nt(jnp.concatenate([Ah.astype(BF16), Rs], axis=1), Z0)
    Uu = AR[:, :2 * C] + U0
    Ys = AR[:, 2 * C:] + bmm(jnp.concatenate([Mrb.astype(BF16), Mrk.astype(BF16)], axis=2),
                             jnp.concatenate([Uu.astype(BF16), Vs], axis=1))
    ysum = Ys[:, :C] + Ys[:, C:]
    for u, (c, pr) in enumerate(units):
        y_s[c * C:(c + 1) * C, pr * LANE:(pr + 1) * LANE] = ysum[u]

    y = y_s[...]
    inv_n = 1.0 / RWKV_HEAD_DIM
    y_mu = _head_sums(y, seg_ref[...]) * inv_n
    d = y - y_mu
    y_var = _head_sums(d * d, seg_ref[...]) * inv_n
    yn = d * lax.rsqrt(y_var + RWKV_GN_EPS) * lng_ref[...] + lnb_ref[...]
    bonus = _head_sums(r_s[...] * k_s[...] * rk_ref[...], seg_ref[...]) * v_s[...]
    o_ref[0] = ((yn + bonus) * g_s[...]).astype(o_ref.dtype)


def _rwkv_branch(x, w_rw, mu, w0, w2p, a0, a2p, k_k, k_a, r_k, lnx_g, lnx_b, seg, ts):
    B, S, D = x.shape
    W = w0.shape[-1]
    P = 3 * W + 2 * RWKV_LORA
    n_pairs = W // LANE
    row = lambda n: _const_spec((1, n))
    scr = lambda: pltpu.VMEM((ts, W), F32)
    return pl.pallas_call(
        functools.partial(_rwkv_kernel, ts=ts, width=W),
        grid=(B, S // ts),
        in_specs=[pl.BlockSpec((1, ts, D), lambda b, s: (b, s, 0)),
                  _const_spec(w_rw.shape), row(P), row(W), _const_spec(w2p.shape), row(W),
                  _const_spec(a2p.shape), row(W), row(W), row(W), row(W), row(W), _const_spec(seg.shape)],
        out_specs=pl.BlockSpec((1, ts, W), lambda b, s: (b, s, 0)),
        out_shape=jax.ShapeDtypeStruct((B, S, W), BF16),
        scratch_shapes=[pltpu.VMEM((8, P), F32), pltpu.VMEM((n_pairs, LANE, LANE), F32),
                        scr(), scr(), scr(), scr(), scr(), scr(), scr(), scr(), scr(), scr(), scr(), scr()],
        compiler_params=_params("arbitrary", "arbitrary"),
        name="rwkv_branch",
    )(x, w_rw, mu, w0, w2p, a0, a2p, k_k, k_a, r_k, lnx_g, lnx_b, seg)


def _mla_prep_kernel(x_ref, w_ref, qn_ref, kvn_ref, wq_ref, wkv_ref, cos_ref, sin_ref,
                     q_ref, k_ref, vt_ref, g_ref, *, q_lora, kv_lora, heads, scale):
    h = jnp.dot(x_ref[0].astype(BF16), w_ref[...], preferred_element_type=F32)
    ql = h[:, :q_lora]
    kvl = h[:, q_lora:q_lora + kv_lora]
    o = q_lora + kv_lora
    kpe, kpe_rot = h[:, o:o + LANE], h[:, o + LANE:o + 2 * LANE]
    gate = h[:, o + 2 * LANE:]
    cos, sin = cos_ref[0], sin_ref[0]
    qn = ql * lax.rsqrt(jnp.mean(ql * ql, axis=-1, keepdims=True) + RMS_EPS) * qn_ref[...]
    kvn = kvl * lax.rsqrt(jnp.mean(kvl * kvl, axis=-1, keepdims=True) + RMS_EPS) * kvn_ref[...]
    q2 = _dot(qn, wq_ref[...])
    kv = _dot(kvn, wkv_ref[...])
    hw = heads * LANE
    k_pe = kpe * cos + kpe_rot * sin
    ones_lane = jnp.where(lax.broadcasted_iota(jnp.int32, (1, LANE), 1) == MLA_V, 1.0, 0.0)
    ts = h.shape[0]
    for i in range(heads):
        ls = slice(i * LANE, (i + 1) * LANE)
        q_ref[0, :, ls] = ((q2[:, ls] * cos + q2[:, hw + i * LANE:hw + (i + 1) * LANE] * sin) * scale).astype(BF16)
        k_ref[0, :, ls] = (kv[:, ls] + k_pe).astype(BF16)
        v_t = (kv[:, hw + i * LANE:hw + (i + 1) * LANE] + ones_lane).T
        for t in range(ts // ATT_TK):
            vt_ref[0, i, t] = v_t[:ATT_VROWS, t * ATT_TK:(t + 1) * ATT_TK].astype(BF16)
    g_ref[0] = _silu(gate).astype(BF16)


def _mla_prep(x, w_mla, q_norm, kv_norm, wq2, wkv2, cos, sin, heads, ts):
    B, S, D = x.shape
    q_lora, kv_lora = q_norm.shape[-1], kv_norm.shape[-1]
    W = heads * MLA_V
    tile = lambda n: pl.BlockSpec((1, ts, n), lambda b, s: (b, s, 0))
    return pl.pallas_call(
        functools.partial(_mla_prep_kernel, q_lora=q_lora, kv_lora=kv_lora, heads=heads,
                          scale=float(MLA_NOPE + MLA_ROPE) ** -0.5 * LOG2_E),
        grid=(B, S // ts),
        in_specs=[tile(D), _const_spec(w_mla.shape), _const_spec((1, q_lora)), _const_spec((1, kv_lora)),
                  _const_spec(wq2.shape), _const_spec(wkv2.shape), tile(LANE), tile(LANE)],
        out_specs=[tile(heads * LANE), tile(heads * LANE),
                   pl.BlockSpec((1, heads, ts // ATT_TK, ATT_VROWS, ATT_TK), lambda b, s: (b, 0, s, 0, 0)), tile(W)],
        out_shape=[jax.ShapeDtypeStruct((B, S, heads * LANE), BF16)] * 2
        + [jax.ShapeDtypeStruct((B, heads, S // ATT_TK, ATT_VROWS, ATT_TK), BF16), jax.ShapeDtypeStruct((B, S, W), BF16)],
        compiler_params=_params("arbitrary", "arbitrary"),
        name="mla_prep",
    )(x, w_mla, q_norm, kv_norm, wq2, wkv2, cos, sin)


def _mla_attn_kernel(q_ref, k_ref, vt_ref, g_ref, o_ref, m_s, acc_s, s_s, *, tq, qb):
    i = pl.program_id(2)
    tk = ATT_TK
    n_blk = tq // qb
    streams = [(e, hb) for e in range(2) for hb in range(n_blk)]
    key_i = lax.broadcasted_iota(jnp.int32, (tk, qb), 0)
    qry_i = lax.broadcasted_iota(jnp.int32, (tk, qb), 1)

    def scores(j, mode, buf):
        rows_k = pl.ds(pl.multiple_of(j * tk, tk), tk)
        for idx, (e, hb) in enumerate(streams):
            if mode[hb] is not None:
                ls = slice(e * LANE, (e + 1) * LANE)
                s_s[buf, idx] = lax.dot_general(k_ref[0, rows_k, ls], q_ref[0, hb * qb:(hb + 1) * qb, ls],
                                                (((1,), (1,)), ((), ())), preferred_element_type=F32)

    def consume(j, mode, buf):
        for idx, (e, hb) in enumerate(streams):
            if mode[hb] is None:
                continue
            s = s_s[buf, idx]
            if mode[hb] is not False:
                s = jnp.where(key_i + mode[hb] <= qry_i, s, -1e30)
            m = m_s[idx, 0:1, :]
            m_new = jnp.maximum(m, jnp.max(s, axis=0, keepdims=True))
            p = jnp.exp2(s - m_new)
            acc_s[idx] = jnp.exp2(m - m_new) * acc_s[idx] + jnp.dot(vt_ref[0, e, j], p.astype(BF16),
                                                                    preferred_element_type=F32)
            m_s[idx, 0:1, :] = m_new

    tiles_per_blk = qb // tk
    n_diag = tq // tk
    assert n_diag % 2 == 0
    first_diag = i * n_diag
    visible = [False] * n_blk

    def diag_mode(d):
        mode = []
        for hb in range(n_blk):
            rel = d - hb * tiles_per_blk
            mode.append(False if rel < 0 else (None if rel >= tiles_per_blk else rel * tk))
        return mode

    m_s[...] = jnp.full(m_s.shape, -1e30, F32)
    acc_s[...] = jnp.zeros(acc_s.shape, F32)
    scores(0, visible, 0)

    def body(jj, carry):
        for u in range(n_diag):
            scores(n_diag * jj + u + 1, visible, (u + 1) % 2)
            consume(n_diag * jj + u, visible, u % 2)
        return carry

    lax.fori_loop(0, i, body, 0)
    for d in range(n_diag):
        if d + 1 < n_diag:
            scores(first_diag + d + 1, diag_mode(d + 1), (d + 1) % 2)
        consume(first_diag + d, diag_mode(d), d % 2)

    for hb in range(n_blk):
        rows = slice(hb * qb, (hb + 1) * qb)
        a0, a1 = acc_s[streams.index((0, hb))], acc_s[streams.index((1, hb))]
        o_t = jnp.concatenate([a0[:MLA_V] / a0[MLA_V:MLA_V + 1], a1[:MLA_V] / a1[MLA_V:MLA_V + 1]], axis=0)
        o_ref[0, rows, :] = (o_t.T * g_ref[0, rows, :].astype(F32)).astype(o_ref.dtype)


def _mla_attention(q, k, vt, g, tq, qb):
    B, S, HW = q.shape
    W = g.shape[-1]
    n_pairs = W // LANE
    n_streams = 2 * (tq // qb)
    return pl.pallas_call(
        functools.partial(_mla_attn_kernel, tq=tq, qb=qb),
        grid=(B, n_pairs, S // tq),
        in_specs=[pl.BlockSpec((1, tq, 2 * LANE), lambda b, p, i: (b, i, p)),
                  pl.BlockSpec((1, S, 2 * LANE), lambda b, p, i: (b, 0, p)),
                  pl.BlockSpec((1, 2, S // ATT_TK, ATT_VROWS, ATT_TK), lambda b, p, i: (b, p, 0, 0, 0)),
                  pl.BlockSpec((1, tq, LANE), lambda b, p, i: (b, i, p))],
        out_specs=pl.BlockSpec((1, tq, LANE), lambda b, p, i: (b, i, p)),
        out_shape=jax.ShapeDtypeStruct((B, S, W), BF16),
        scratch_shapes=[pltpu.VMEM((n_streams, SUBLANE, qb), F32), pltpu.VMEM((n_streams, ATT_VROWS, qb), F32),
                        pltpu.VMEM((2, n_streams, ATT_TK, qb), F32)],
        compiler_params=_params("arbitrary", "arbitrary", "arbitrary"),
        name="mla_attention",
    )(q, k, vt, g)


def _conv_compute(xb, w_ref, cw_ref, cb_ref, lg_ref, lb_ref, hbuf, y_s, *, ts, width):
    W = width
    s_idx = pl.program_id(1)

    @pl.when(s_idx == 0)
    def _():
        hbuf[0, 0:CONV_HALO, :] = jnp.zeros((CONV_HALO, W), F32)

    @pl.when(s_idx > 0)
    def _():
        hbuf[0, 0:CONV_HALO, :] = hbuf[0, ts:ts + CONV_HALO, :]

    h = jnp.dot(xb, w_ref[...], preferred_element_type=F32)
    hbuf[0, CONV_HALO:CONV_HALO + ts, :] = h[:, :W] * _sigmoid(h[:, W:2 * W])
    shifted_rows = ts + CONV_HALO - SUBLANE
    for r in range(1, SUBLANE):
        hbuf[r, 0:shifted_rows, :] = hbuf[0, r:r + shifted_rows, :]
    first_tap = CONV_HALO - (CONV_KERNEL - 1)
    acc = jnp.zeros((ts, W), F32) + cb_ref[...]
    for j in range(CONV_KERNEL):
        q8, r = divmod(first_tap + j, SUBLANE)
        acc = acc + cw_ref[j:j + 1, :] * hbuf[r, q8 * SUBLANE:q8 * SUBLANE + ts, :]
    mu = jnp.mean(acc, axis=-1, keepdims=True)
    d = acc - mu
    var = jnp.mean(d * d, axis=-1, keepdims=True)
    y = _silu(d * lax.rsqrt(var + LN_EPS) * lg_ref[...] + lb_ref[...])
    y_s[...] = (y * _silu(h[:, 2 * W:])).astype(y_s.dtype)


def _xattn_compute(xb, mem_ref, w_ref, wkv_ref, kv_s, y_s, *, width, scale):
    W = width

    @pl.when(pl.program_id(1) == 0)
    def _():
        kv_s[...] = jnp.dot(mem_ref[0].astype(BF16), wkv_ref[...], preferred_element_type=F32).astype(BF16)

    h = jnp.dot(xb, w_ref[...], preferred_element_type=F32)
    for i in range(XATTN_HEADS):
        ls = slice(i * LANE, (i + 1) * LANE)
        s = _dot_nt(h[:, ls] * scale, kv_s[:, ls])
        p = jnp.exp(s - jnp.max(s, axis=-1, keepdims=True))
        o = _dot(p, kv_s[:, W + i * LANE:W + (i + 1) * LANE]) / jnp.sum(p, axis=-1, keepdims=True)
        y_s[:, ls] = (o * _silu(h[:, W + i * LANE:W + (i + 1) * LANE])).astype(y_s.dtype)


def _tail_kernel(x_ref, y0_ref, y1_ref, mem_ref, wc_ref, cw_ref, cb_ref, clg_ref, clb_ref, wx_ref, wkv_ref,
                 wm_ref, bg_ref, wo_ref, wout_ref, lg_ref, lb_ref, o_ref, hbuf, kv_s, yc_s, ym_s,
                 *, alpha, ts, width):
    x = x_ref[0]
    xb = x.astype(BF16)
    D = x.shape[-1]
    _conv_compute(xb, wc_ref, cw_ref, cb_ref, clg_ref, clb_ref, hbuf, yc_s, ts=ts, width=width)
    _xattn_compute(xb, mem_ref, wx_ref, wkv_ref, kv_s, ym_s, width=width,
                   scale=float(width // XATTN_HEADS) ** -0.5)
    merged = None
    for n, y in enumerate((y0_ref[0], y1_ref[0], yc_s[...], ym_s[...])):
        gate = _sigmoid(jnp.dot(xb, wm_ref[:, n * D:(n + 1) * D], preferred_element_type=F32) + bg_ref[n:n + 1, :])
        term = gate * jnp.dot(y, wo_ref[n], preferred_element_type=F32)
        merged = term if merged is None else merged + term
    z = alpha * x + _dot(merged, wout_ref[...])
    mu = jnp.mean(z, axis=-1, keepdims=True)
    d = z - mu
    var = jnp.mean(d * d, axis=-1, keepdims=True)
    o_ref[0] = d * lax.rsqrt(var + LN_EPS) * lg_ref[...] + lb_ref[...]


def _tail(x, y_rwkv, y_mla, mem, w_conv, conv_w, conv_b, conv_ln_g, conv_ln_b, w_x, w_mem_kv,
          w_merge, b_gate, w_o, w_out, ln_g, ln_b, alpha, ts):
    B, S, D = x.shape
    M = mem.shape[1]
    W = y_rwkv.shape[-1]
    consts = (w_conv, conv_w, conv_b, conv_ln_g, conv_ln_b, w_x, w_mem_kv, w_merge, b_gate, w_o, w_out, ln_g, ln_b)
    tile = lambda n: pl.BlockSpec((1, ts, n), lambda b, s: (b, s, 0))
    return pl.pallas_call(
        functools.partial(_tail_kernel, alpha=alpha, ts=ts, width=W),
        grid=(B, S // ts),
        in_specs=[tile(D), tile(W), tile(W), pl.BlockSpec((1, M, D), lambda b, s: (b, 0, 0))]
        + [_const_spec(a.shape, single_buffer=True) for a in consts],
        out_specs=tile(D),
        out_shape=jax.ShapeDtypeStruct((B, S, D), F32),
        scratch_shapes=[pltpu.VMEM((SUBLANE, ts + CONV_HALO, W), F32), pltpu.VMEM((M, 2 * W), BF16),
                        pltpu.VMEM((ts, W), BF16), pltpu.VMEM((ts, W), BF16)],
        compiler_params=_params("arbitrary", "arbitrary"),
        name="conv_xattn_merge",
    )(x, y_rwkv, y_mla, mem, *consts)


def _rotate_half_cols(w):
    half = w.shape[-1] // 2
    return jnp.concatenate([-w[..., half:], w[..., :half]], axis=-1)


def kernel(x, mem, positions, w_in, b_gate, rwkv_mu, rwkv_w0, rwkv_w2, rwkv_a0, rwkv_a2, rwkv_k_k, rwkv_k_a, rwkv_r_k, rwkv_lnx_g, rwkv_lnx_b, mla_q_norm, mla_w_uq, mla_kv_norm, mla_w_ukv, conv_w, conv_b, conv_ln_g, conv_ln_b, xattn_w_mem_kv, w_o_branch, w_out, ln_g, ln_b):
    B, S, D = x.shape
    depth = w_in.shape[0]
    W = D // 2
    q_lora, kv_lora = mla_q_norm.shape[-1], mla_kv_norm.shape[-1]
    heads = W // MLA_V
    P = 3 * W + 2 * RWKV_LORA
    alpha = (2.0 * depth) ** 0.25
    assert W % LANE == 0 and LANE == MLA_NOPE + MLA_V and MLA_NOPE + MLA_ROPE <= LANE
    ts = min(S, 256)
    qb = min(S, 256)
    tq = min(S, 8 * qb)
    assert S % ts == 0 and ts % CHUNK == 0 and ts % ATT_TK == 0 and S % tq == 0 and tq % qb == 0 and qb % ATT_TK == 0

    sizes = (P, W, q_lora, kv_lora, MLA_ROPE, W, 2 * W, W, W, W, N_BRANCH * D)
    offs = [0]
    for n in sizes:
        offs.append(offs[-1] + n)
    o_rw, _, o_q, o_kv, o_kpe, o_mg, o_cu, _, o_xq, _, o_merge, o_end = offs
    assert o_end == w_in.shape[-1]

    inv_freq = ROPE_THETA ** (-jnp.arange(0, MLA_ROPE, 2, dtype=F32) / MLA_ROPE)
    zeros_f = lambda n: jnp.zeros((n,), F32)
    freq128 = jnp.concatenate([zeros_f(MLA_NOPE), inv_freq, inv_freq, zeros_f(LANE - MLA_NOPE - MLA_ROPE)])[None, :]
    cos, sin = _rope_tables(positions, freq128)

    lane_head = jnp.arange(MXU_WIDTH) // RWKV_HEAD_DIM
    seg = (lane_head[:, None] == lane_head[None, :]).astype(BF16)
    zl = jnp.zeros((RWKV_LORA, W), F32)
    row = lambda a: a.reshape(1, -1)

    for l in range(depth):
        wi = w_in[l]
        w_rw = wi[:, o_rw:o_q].astype(BF16)
        w2p = jnp.concatenate([rwkv_w2[l], zl], axis=0).astype(BF16)
        a2p = jnp.concatenate([zl, rwkv_a2[l]], axis=0).astype(BF16)
        y_rwkv = _rwkv_branch(x, w_rw, row(rwkv_mu[l]), row(rwkv_w0[l]), w2p, row(rwkv_a0[l]), a2p,
                              row(rwkv_k_k[l]), row(rwkv_k_a[l]), row(rwkv_r_k[l]), row(rwkv_lnx_g[l]),
                              row(rwkv_lnx_b[l]), seg, ts)

        kpe_w = wi[:, o_kpe:o_mg]
        pad_l, pad_r = jnp.zeros((D, MLA_NOPE), F32), jnp.zeros((D, LANE - MLA_NOPE - MLA_ROPE), F32)
        w_mla = jnp.concatenate([wi[:, o_q:o_kpe], pad_l, kpe_w, pad_r, pad_l, _rotate_half_cols(kpe_w), pad_r,
                                 wi[:, o_mg:o_cu]], axis=1).astype(BF16)
        wq = mla_w_uq[l].reshape(q_lora, heads, MLA_NOPE + MLA_ROPE)
        q_nope, q_pe = wq[..., :MLA_NOPE], wq[..., MLA_NOPE:]
        zq = lambda n: jnp.zeros((q_lora, heads, n), F32)
        wq_plain = jnp.concatenate([q_nope, q_pe, zq(LANE - MLA_NOPE - MLA_ROPE)], axis=-1)
        wq_rot = jnp.concatenate([zq(MLA_NOPE), _rotate_half_cols(q_pe), zq(LANE - MLA_NOPE - MLA_ROPE)], axis=-1)
        wq2 = jnp.concatenate([wq_plain.reshape(q_lora, -1), wq_rot.reshape(q_lora, -1)], axis=1).astype(BF16)
        wkv = mla_w_ukv[l].reshape(kv_lora, heads, MLA_NOPE + MLA_V)
        wk = jnp.concatenate([wkv[..., :MLA_NOPE], jnp.zeros((kv_lora, heads, LANE - MLA_NOPE), F32)], axis=-1)
        wv = jnp.concatenate([wkv[..., MLA_NOPE:], jnp.zeros((kv_lora, heads, LANE - MLA_V), F32)], axis=-1)
        wkv2 = jnp.concatenate([wk.reshape(kv_lora, -1), wv.reshape(kv_lora, -1)], axis=1).astype(BF16)
        q, k, vt, g = _mla_prep(x, w_mla, row(mla_q_norm[l]), row(mla_kv_norm[l]), wq2, wkv2, cos, sin, heads, min(S, 2 * ts))
        y_mla = _mla_attention(q, k, vt, g, tq, qb)

        x = _tail(x, y_rwkv, y_mla, mem, wi[:, o_cu:o_xq].astype(BF16), conv_w[l], row(conv_b[l]),
                  row(conv_ln_g[l]), row(conv_ln_b[l]), wi[:, o_xq:o_merge].astype(BF16),
                  xattn_w_mem_kv[l].astype(BF16), wi[:, o_merge:].astype(BF16), b_gate[l],
                  w_o_branch[l].astype(BF16), w_out[l].astype(BF16), row(ln_g[l]), row(ln_b[l]), alpha, min(S, 2 * ts))
    return x
```

```python
import functools

import jax
import jax.numpy as jnp
from jax import lax
from jax.experimental import pallas as pl
from jax.experimental.pallas import tpu as pltpu

F32 = jnp.float32
BF16 = jnp.bfloat16

N_BRANCH = 4
RWKV_HEAD_DIM = 64
RWKV_LORA = 64
RWKV_GN_EPS = 64e-5
MLA_NOPE = 64
MLA_ROPE = 32
MLA_V = 64
ROPE_THETA = 10000.0
CONV_KERNEL = 31
XATTN_HEADS = 4
LN_EPS = 1e-5
RMS_EPS = 1e-6

LANE = 128
SUBLANE = 8
MXU_WIDTH = 256
CHUNK = 64
CONV_HALO = 32
ATT_TK = 256
ATT_VROWS = 80
LOG2_E = 1.4426950408889634
VMEM_LIMIT = 56 * 1024 * 1024


def _dot(a, b):
    return jnp.dot(a.astype(BF16), b.astype(BF16), preferred_element_type=F32)


def _dot_nt(a, b):
    return lax.dot_general(a.astype(BF16), b.astype(BF16), (((1,), (1,)), ((), ())),
                           preferred_element_type=F32)


def _split_hi_lo(a):
    hi = a.astype(BF16)
    lo = (a - hi.astype(F32)).astype(BF16)
    return hi, lo


def _dot_0_1_rhs(a, ones_like_matrix):
    hi, lo = _split_hi_lo(a)
    return (jnp.dot(hi, ones_like_matrix, preferred_element_type=F32)
            + jnp.dot(lo, ones_like_matrix, preferred_element_type=F32))


def _head_sums(a, seg):
    g = seg.shape[0]
    return jnp.concatenate([_dot_0_1_rhs(a[:, o:o + g], seg) for o in range(0, a.shape[1], g)], axis=1)


def _sigmoid(x):
    return 1.0 / (1.0 + jnp.exp(-x))


def _silu(x):
    return x * _sigmoid(x)


def _const_spec(shape, single_buffer=False):
    nd = len(shape)
    mode = pl.Buffered(1) if single_buffer else None
    return pl.BlockSpec(shape, lambda *_: (0,) * nd, pipeline_mode=mode)


def _params(*sem):
    return pltpu.CompilerParams(dimension_semantics=sem, vmem_limit_bytes=VMEM_LIMIT)


def _rope_kernel(pos_ref, f_ref, cos_ref, sin_ref):
    ang = pos_ref[0].astype(F32) * f_ref[...]
    cos_ref[0] = jnp.cos(ang)
    sin_ref[0] = jnp.sin(ang)


def _rope_tables(positions, freq128):
    B, S = positions.shape
    return pl.pallas_call(
        _rope_kernel,
        grid=(B,),
        in_specs=[pl.BlockSpec((1, S, 1), lambda b: (b, 0, 0)), _const_spec((1, LANE))],
        out_specs=[pl.BlockSpec((1, S, LANE), lambda b: (b, 0, 0))] * 2,
        out_shape=[jax.ShapeDtypeStruct((B, S, LANE), F32)] * 2,
        compiler_params=_params("arbitrary"),
        name="rope_tables",
    )(positions[..., None], freq128)


def _rwkv_kernel(x_ref, w_ref, mu_ref, w0_ref, w2_ref, a0_ref, a2_ref, kk_ref, ka_ref, rk_ref,
                 lng_ref, lnb_ref, seg_ref, o_ref,
                 carry_ref, z_ref, r_s, k_s, v_s, g_s, at_s, rt_s, bt_s, kt_s, bh_s, kh_s, et_s, y_s, *, ts, width):
    W = width
    P = 3 * W + 2 * RWKV_LORA
    n_pairs = W // LANE
    s_idx = pl.program_id(1)

    @pl.when(s_idx == 0)
    def _():
        carry_ref[...] = jnp.zeros_like(carry_ref)
        z_ref[...] = jnp.zeros_like(z_ref)

    h = jnp.dot(x_ref[0].astype(BF16), w_ref[...], preferred_element_type=F32)
    hp = h[:, :P]
    rows = lax.broadcasted_iota(jnp.int32, hp.shape, 0)
    prev = jnp.where(rows == 0, carry_ref[0:1, :], pltpu.roll(hp, 1, 0))
    carry_ref[0:1, :] = hp[ts - 1:ts, :]
    p = hp + mu_ref[...] * (prev - hp)
    r, k, v, wa = p[:, :W], p[:, W:2 * W], p[:, 2 * W:3 * W], p[:, 3 * W:]

    zz = -(w0_ref[...] + _dot(jnp.tanh(wa), w2_ref[...]))
    softplus = jnp.maximum(zz, 0.0) + jnp.log(1.0 + jnp.exp(-jnp.abs(zz)))
    lw = -jnp.exp(-softplus - 0.5)
    a = _sigmoid(a0_ref[...] + _dot(wa, a2_ref[...]))
    kk = k * kk_ref[...]
    ss = _head_sums(kk * kk, seg_ref[...])
    kkn = kk * lax.rsqrt(jnp.maximum(ss, 1e-24))
    kmod = k * (1.0 + (a - 1.0) * ka_ref[...])
    b = kkn * a

    C = CHUNK
    n_chunks = ts // C
    rr = lax.broadcasted_iota(jnp.int32, (ts, ts), 0)
    cc = lax.broadcasted_iota(jnp.int32, (ts, ts), 1)
    same_chunk = (rr // C) == (cc // C)
    lw_hi, lw_lo = _split_hi_lo(lw)
    tri = jnp.where(same_chunk & (rr >= cc), 1.0, 0.0).astype(BF16)
    blk = jnp.where(same_chunk, 1.0, 0.0).astype(BF16)
    cs = jnp.dot(tri, lw_hi, preferred_element_type=F32) + jnp.dot(tri, lw_lo, preferred_element_type=F32)
    tot = jnp.dot(blk, lw_hi, preferred_element_type=F32) + jnp.dot(blk, lw_lo, preferred_element_type=F32)
    w_inv, w_end = jnp.exp(-cs), jnp.exp(tot - cs)
    r_s[...] = r
    k_s[...] = kmod
    v_s[...] = v
    g_s[...] = _silu(h[:, P:])
    at_s[...] = -kkn * jnp.exp(cs - lw)
    rt_s[...] = r * jnp.exp(cs)
    bt_s[...] = b * w_inv
    kt_s[...] = kmod * w_inv
    bh_s[...] = b * w_end
    kh_s[...] = kmod * w_end
    et_s[...] = jnp.exp(tot)

    units = [(c, pr) for c in range(n_chunks) for pr in range(n_pairs)]
    first_head = lax.broadcasted_iota(jnp.int32, (C, LANE), 1) < RWKV_HEAD_DIM

    def stacked(ref):
        out = []
        for c, pr in units:
            t = ref[c * C:(c + 1) * C, pr * LANE:(pr + 1) * LANE]
            out.append(jnp.concatenate([jnp.where(first_head, t, 0.0), jnp.where(first_head, 0.0, t)],
                                       axis=0).astype(BF16))
        return jnp.stack(out)

    def bmm(x, y):
        return jnp.einsum('umk,ukn->umn', x.astype(BF16), y.astype(BF16), preferred_element_type=F32)

    def bmm_nt(x, y):
        return jnp.einsum('umk,unk->umn', x.astype(BF16), y.astype(BF16), preferred_element_type=F32)

    def bmm_tn(x, y):
        return jnp.einsum('ukm,ukn->umn', x.astype(BF16), y.astype(BF16), preferred_element_type=F32)

    ri = lax.broadcasted_iota(jnp.int32, (2 * C, 2 * C), 0)
    ci = lax.broadcasted_iota(jnp.int32, (2 * C, 2 * C), 1)
    strict = (ri > ci)[None]
    incl = (ri >= ci)[None]
    eye = (ri == ci).astype(F32)[None]

    As, Rs, Bs, Ks = stacked(at_s), stacked(rt_s), stacked(bt_s), stacked(kt_s)
    Vs, Bh, Kh = stacked(v_s), stacked(bh_s), stacked(kh_s)
    G = bmm_nt(jnp.concatenate([As, Rs], axis=1), jnp.concatenate([Bs, Ks], axis=1))
    L = jnp.where(strict, G[:, :2 * C, :2 * C], 0.0)
    Lak = jnp.where(strict, G[:, :2 * C, 2 * C:], 0.0)
    Mrb = jnp.where(incl, G[:, 2 * C:, :2 * C], 0.0)
    Mrk = jnp.where(incl, G[:, 2 * C:, 2 * C:], 0.0)
    T = eye + L
    Lb = L.astype(BF16)
    Pw = bmm(Lb, Lb)
    step = 2
    while step * 2 < C:
        Pb = Pw.astype(BF16)
        R2 = bmm(Pb, jnp.concatenate([Pb, T.astype(BF16)], axis=2))
        Pw, T = R2[:, :, :2 * C], T + R2[:, :, 2 * C:]
        step *= 2
    T = T + bmm(Pw, T)
    X = bmm(Lak, Vs)
    AU = bmm(T, jnp.concatenate([As, X.astype(BF16)], axis=2))
    Ah, U0 = AU[:, :, :LANE], AU[:, :, LANE:]
    Gc = bmm_tn(Ah, Bh)
    Hc = bmm_tn(jnp.concatenate([U0.astype(BF16), Vs], axis=1), jnp.concatenate([Bh, Kh], axis=1))
    Z = z_ref[...]
    starts = []
    for c in range(n_chunks):
        us = slice(c * n_pairs, (c + 1) * n_pairs)
        e_tot = jnp.stack([et_s[c * C:c * C + 1, pr * LANE:(pr + 1) * LANE] for pr in range(n_pairs)])
        starts.append(Z)
        Z = Z * e_tot + bmm(Z, Gc[us]) + Hc[us]
    z_ref[...] = Z
    Z0 = jnp.concatenate(starts, axis=0)
    AR = bmm_nt(jnp.concatenate([Ah.astype(BF16), Rs], axis=1), Z0)
    Uu = AR[:, :2 * C] + U0
    Ys = AR[:, 2 * C:] + bmm(jnp.concatenate([Mrb.astype(BF16), Mrk.astype(BF16)], axis=2),
                             jnp.concatenate([Uu.astype(BF16), Vs], axis=1))
    ysum = Ys[:, :C] + Ys[:, C:]
    for u, (c, pr) in enumerate(units):
        y_s[c * C:(c + 1) * C, pr * LANE:(pr + 1) * LANE] = ysum[u]

    y = y_s[...]
    inv_n = 1.0 / RWKV_HEAD_DIM
    y_mu = _head_sums(y, seg_ref[...]) * inv_n
    d = y - y_mu
    y_var = _head_sums(d * d, seg_ref[...]) * inv_n
    yn = d * lax.rsqrt(y_var + RWKV_GN_EPS) * lng_ref[...] + lnb_ref[...]
    bonus = _head_sums(r_s[...] * k_s[...] * rk_ref[...], seg_ref[...]) * v_s[...]
    o_ref[0] = ((yn + bonus) * g_s[...]).astype(o_ref.dtype)


def _rwkv_branch(x, w_rw, mu, w0, w2p, a0, a2p, k_k, k_a, r_k, lnx_g, lnx_b, seg, ts):
    B, S, D = x.shape
    W = w0.shape[-1]
    P = 3 * W + 2 * RWKV_LORA
    n_pairs = W // LANE
    row = lambda n: _const_spec((1, n))
    scr = lambda: pltpu.VMEM((ts, W), F32)
    return pl.pallas_call(
        functools.partial(_rwkv_kernel, ts=ts, width=W),
        grid=(B, S // ts),
        in_specs=[pl.BlockSpec((1, ts, D), lambda b, s: (b, s, 0)),
                  _const_spec(w_rw.shape), row(P), row(W), _const_spec(w2p.shape), row(W),
                  _const_spec(a2p.shape), row(W), row(W), row(W), row(W), row(W), _const_spec(seg.shape)],
        out_specs=pl.BlockSpec((1, ts, W), lambda b, s: (b, s, 0)),
        out_shape=jax.ShapeDtypeStruct((B, S, W), BF16),
        scratch_shapes=[pltpu.VMEM((8, P), F32), pltpu.VMEM((n_pairs, LANE, LANE), F32),
                        scr(), scr(), scr(), scr(), scr(), scr(), scr(), scr(), scr(), scr(), scr(), scr()],
        compiler_params=_params("arbitrary", "arbitrary"),
        name="rwkv_branch",
    )(x, w_rw, mu, w0, w2p, a0, a2p, k_k, k_a, r_k, lnx_g, lnx_b, seg)


def _mla_prep_kernel(x_ref, w_ref, qn_ref, kvn_ref, wq_ref, wkv_ref, cos_ref, sin_ref,
                     q_ref, k_ref, vt_ref, g_ref, *, q_lora, kv_lora, heads, scale):
    h = jnp.dot(x_ref[0].astype(BF16), w_ref[...], preferred_element_type=F32)
    ql = h[:, :q_lora]
    kvl = h[:, q_lora:q_lora + kv_lora]
    o = q_lora + kv_lora
    kpe, kpe_rot = h[:, o:o + LANE], h[:, o + LANE:o + 2 * LANE]
    gate = h[:, o + 2 * LANE:]
    cos, sin = cos_ref[0], sin_ref[0]
    qn = ql * lax.rsqrt(jnp.mean(ql * ql, axis=-1, keepdims=True) + RMS_EPS) * qn_ref[...]
    kvn = kvl * lax.rsqrt(jnp.mean(kvl * kvl, axis=-1, keepdims=True) + RMS_EPS) * kvn_ref[...]
    q2 = _dot(qn, wq_ref[...])
    kv = _dot(kvn, wkv_ref[...])
    hw = heads * LANE
    k_pe = kpe * cos + kpe_rot * sin
    ones_lane = jnp.where(lax.broadcasted_iota(jnp.int32, (1, LANE), 1) == MLA_V, 1.0, 0.0)
    ts = h.shape[0]
    for i in range(heads):
        ls = slice(i * LANE, (i + 1) * LANE)
        q_ref[0, :, ls] = ((q2[:, ls] * cos + q2[:, hw + i * LANE:hw + (i + 1) * LANE] * sin) * scale).astype(BF16)
        k_ref[0, :, ls] = (kv[:, ls] + k_pe).astype(BF16)
        v_t = (kv[:, hw + i * LANE:hw + (i + 1) * LANE] + ones_lane).T
        for t in range(ts // ATT_TK):
            vt_ref[0, i, t] = v_t[:ATT_VROWS, t * ATT_TK:(t + 1) * ATT_TK].astype(BF16)
    g_ref[0] = _silu(gate).astype(BF16)


def _mla_prep(x, w_mla, q_norm, kv_norm, wq2, wkv2, cos, sin, heads, ts):
    B, S, D = x.shape
    q_lora, kv_lora = q_norm.shape[-1], kv_norm.shape[-1]
    W = heads * MLA_V
    tile = lambda n: pl.BlockSpec((1, ts, n), lambda b, s: (b, s, 0))
    return pl.pallas_call(
        functools.partial(_mla_prep_kernel, q_lora=q_lora, kv_lora=kv_lora, heads=heads,
                          scale=float(MLA_NOPE + MLA_ROPE) ** -0.5 * LOG2_E),
        grid=(B, S // ts),
        in_specs=[tile(D), _const_spec(w_mla.shape), _const_spec((1, q_lora)), _const_spec((1, kv_lora)),
                  _const_spec(wq2.shape), _const_spec(wkv2.shape), tile(LANE), tile(LANE)],
        out_specs=[tile(heads * LANE), tile(heads * LANE),
                   pl.BlockSpec((1, heads, ts // ATT_TK, ATT_VROWS, ATT_TK), lambda b, s: (b, 0, s, 0, 0)), tile(W)],
        out_shape=[jax.ShapeDtypeStruct((B, S, heads * LANE), BF16)] * 2
        + [jax.ShapeDtypeStruct((B, heads, S // ATT_TK, ATT_VROWS, ATT_TK), BF16), jax.ShapeDtypeStruct((B, S, W), BF16)],
        compiler_params=_params("arbitrary", "arbitrary"),
        name="mla_prep",
    )(x, w_mla, q_norm, kv_norm, wq2, wkv2, cos, sin)


def _mla_attn_kernel(q_ref, k_ref, vt_ref, g_ref, o_ref, m_s, acc_s, s_s, *, tq, qb):
    i = pl.program_id(2)
    tk = ATT_TK
    n_blk = tq // qb
    streams = [(e, hb) for e in range(2) for hb in range(n_blk)]
    key_i = lax.broadcasted_iota(jnp.int32, (tk, qb), 0)
    qry_i = lax.broadcasted_iota(jnp.int32, (tk, qb), 1)

    def scores(j, mode, buf):
        rows_k = pl.ds(pl.multiple_of(j * tk, tk), tk)
        for idx, (e, hb) in enumerate(streams):
            if mode[hb] is not None:
                ls = slice(e * LANE, (e + 1) * LANE)
                s_s[buf, idx] = lax.dot_general(k_ref[0, rows_k, ls], q_ref[0, hb * qb:(hb + 1) * qb, ls],
                                                (((1,), (1,)), ((), ())), preferred_element_type=F32)

    def consume(j, mode, buf):
        for idx, (e, hb) in enumerate(streams):
            if mode[hb] is None:
                continue
            s = s_s[buf, idx]
            if mode[hb] is not False:
                s = jnp.where(key_i + mode[hb] <= qry_i, s, -1e30)
            m = m_s[idx, 0:1, :]
            m_new = jnp.maximum(m, jnp.max(s, axis=0, keepdims=True))
            p = jnp.exp2(s - m_new)
            acc_s[idx] = jnp.exp2(m - m_new) * acc_s[idx] + jnp.dot(vt_ref[0, e, j], p.astype(BF16),
                                                                    preferred_element_type=F32)
            m_s[idx, 0:1, :] = m_new

    tiles_per_blk = qb // tk
    n_diag = tq // tk
    assert n_diag % 2 == 0
    first_diag = i * n_diag
    visible = [False] * n_blk

    def diag_mode(d):
        mode = []
        for hb in range(n_blk):
            rel = d - hb * tiles_per_blk
            mode.append(False if rel < 0 else (None if rel >= tiles_per_blk else rel * tk))
        return mode

    m_s[...] = jnp.full(m_s.shape, -1e30, F32)
    acc_s[...] = jnp.zeros(acc_s.shape, F32)
    scores(0, visible, 0)

    def body(jj, carry):
        for u in range(n_diag):
            scores(n_diag * jj + u + 1, visible, (u + 1) % 2)
            consume(n_diag * jj + u, visible, u % 2)
        return carry

    lax.fori_loop(0, i, body, 0)
    for d in range(n_diag):
        if d + 1 < n_diag:
            scores(first_diag + d + 1, diag_mode(d + 1), (d + 1) % 2)
        consume(first_diag + d, diag_mode(d), d % 2)

    for hb in range(n_blk):
        rows = slice(hb * qb, (hb + 1) * qb)
        a0, a1 = acc_s[streams.index((0, hb))], acc_s[streams.index((1, hb))]
        o_t = jnp.concatenate([a0[:MLA_V] / a0[MLA_V:MLA_V + 1], a1[:MLA_V] / a1[MLA_V:MLA_V + 1]], axis=0)
        o_ref[0, rows, :] = (o_t.T * g_ref[0, rows, :].astype(F32)).astype(o_ref.dtype)


def _mla_attention(q, k, vt, g, tq, qb):
    B, S, HW = q.shape
    W = g.shape[-1]
    n_pairs = W // LANE
    n_streams = 2 * (tq // qb)
    return pl.pallas_call(
        functools.partial(_mla_attn_kernel, tq=tq, qb=qb),
        grid=(B, n_pairs, S // tq),
        in_specs=[pl.BlockSpec((1, tq, 2 * LANE), lambda b, p, i: (b, i, p)),
                  pl.BlockSpec((1, S, 2 * LANE), lambda b, p, i: (b, 0, p)),
                  pl.BlockSpec((1, 2, S // ATT_TK, ATT_VROWS, ATT_TK), lambda b, p, i: (b, p, 0, 0, 0)),
                  pl.BlockSpec((1, tq, LANE), lambda b, p, i: (b, i, p))],
        out_specs=pl.BlockSpec((1, tq, LANE), lambda b, p, i: (b, i, p)),
        out_shape=jax.ShapeDtypeStruct((B, S, W), BF16),
        scratch_shapes=[pltpu.VMEM((n_streams, SUBLANE, qb), F32), pltpu.VMEM((n_streams, ATT_VROWS, qb), F32),
                        pltpu.VMEM((2, n_streams, ATT_TK, qb), F32)],
        compiler_params=_params("arbitrary", "arbitrary", "arbitrary"),
        name="mla_attention",
    )(q, k, vt, g)


def _conv_compute(xb, w_ref, cw_ref, cb_ref, lg_ref, lb_ref, hbuf, y_s, *, ts, width):
    W = width
    s_idx = pl.program_id(1)

    @pl.when(s_idx == 0)
    def _():
        hbuf[0, 0:CONV_HALO, :] = jnp.zeros((CONV_HALO, W), F32)

    @pl.when(s_idx > 0)
    def _():
        hbuf[0, 0:CONV_HALO, :] = hbuf[0, ts:ts + CONV_HALO, :]

    h = jnp.dot(xb, w_ref[...], preferred_element_type=F32)
    hbuf[0, CONV_HALO:CONV_HALO + ts, :] = h[:, :W] * _sigmoid(h[:, W:2 * W])
    shifted_rows = ts + CONV_HALO - SUBLANE
    for r in range(1, SUBLANE):
        hbuf[r, 0:shifted_rows, :] = hbuf[0, r:r + shifted_rows, :]
    first_tap = CONV_HALO - (CONV_KERNEL - 1)
    acc = jnp.zeros((ts, W), F32) + cb_ref[...]
    for j in range(CONV_KERNEL):
        q8, r = divmod(first_tap + j, SUBLANE)
        acc = acc + cw_ref[j:j + 1, :] * hbuf[r, q8 * SUBLANE:q8 * SUBLANE + ts, :]
    mu = jnp.mean(acc, axis=-1, keepdims=True)
    d = acc - mu
    var = jnp.mean(d * d, axis=-1, keepdims=True)
    y = _silu(d * lax.rsqrt(var + LN_EPS) * lg_ref[...] + lb_ref[...])
    y_s[...] = (y * _silu(h[:, 2 * W:])).astype(y_s.dtype)


def _xattn_compute(xb, mem_ref, w_ref, wkv_ref, kv_s, y_s, *, width, scale):
    W = width

    @pl.when(pl.program_id(1) == 0)
    def _():
        kv_s[...] = jnp.dot(mem_ref[0].astype(BF16), wkv_ref[...], preferred_element_type=F32).astype(BF16)

    h = jnp.dot(xb, w_ref[...], preferred_element_type=F32)
    for i in range(XATTN_HEADS):
        ls = slice(i * LANE, (i + 1) * LANE)
        s = _dot_nt(h[:, ls] * scale, kv_s[:, ls])
        p = jnp.exp(s - jnp.max(s, axis=-1, keepdims=True))
        o = _dot(p, kv_s[:, W + i * LANE:W + (i + 1) * LANE]) / jnp.sum(p, axis=-1, keepdims=True)
        y_s[:, ls] = (o * _silu(h[:, W + i * LANE:W + (i + 1) * LANE])).astype(y_s.dtype)


def _tail_kernel(x_ref, y0_ref, y1_ref, mem_ref, wc_ref, cw_ref, cb_ref, clg_ref, clb_ref, wx_ref, wkv_ref,
                 wm_ref, bg_ref, wo_ref, wout_ref, lg_ref, lb_ref, o_ref, hbuf, kv_s, yc_s, ym_s,
                 *, alpha, ts, width):
    x = x_ref[0]
    xb = x.astype(BF16)
    D = x.shape[-1]
    _conv_compute(xb, wc_ref, cw_ref, cb_ref, clg_ref, clb_ref, hbuf, yc_s, ts=ts, width=width)
    _xattn_compute(xb, mem_ref, wx_ref, wkv_ref, kv_s, ym_s, width=width,
                   scale=float(width // XATTN_HEADS) ** -0.5)
    merged = None
    for n, y in enumerate((y0_ref[0], y1_ref[0], yc_s[...], ym_s[...])):
        gate = _sigmoid(jnp.dot(xb, wm_ref[:, n * D:(n + 1) * D], preferred_element_type=F32) + bg_ref[n:n + 1, :])
        term = gate * jnp.dot(y, wo_ref[n], preferred_element_type=F32)
        merged = term if merged is None else merged + term
    z = alpha * x + _dot(merged, wout_ref[...])
    mu = jnp.mean(z, axis=-1, keepdims=True)
    d = z - mu
    var = jnp.mean(d * d, axis=-1, keepdims=True)
    o_ref[0] = d * lax.rsqrt(var + LN_EPS) * lg_ref[...] + lb_ref[...]


def _tail(x, y_rwkv, y_mla, mem, w_conv, conv_w, conv_b, conv_ln_g, conv_ln_b, w_x, w_mem_kv,
          w_merge, b_gate, w_o, w_out, ln_g, ln_b, alpha, ts):
    B, S, D = x.shape
    M = mem.shape[1]
    W = y_rwkv.shape[-1]
    consts = (w_conv, conv_w, conv_b, conv_ln_g, conv_ln_b, w_x, w_mem_kv, w_merge, b_gate, w_o, w_out, ln_g, ln_b)
    tile = lambda n: pl.BlockSpec((1, ts, n), lambda b, s: (b, s, 0))
    return pl.pallas_call(
        functools.partial(_tail_kernel, alpha=alpha, ts=ts, width=W),
        grid=(B, S // ts),
        in_specs=[tile(D), tile(W), tile(W), pl.BlockSpec((1, M, D), lambda b, s: (b, 0, 0))]
        + [_const_spec(a.shape, single_buffer=True) for a in consts],
        out_specs=tile(D),
        out_shape=jax.ShapeDtypeStruct((B, S, D), F32),
        scratch_shapes=[pltpu.VMEM((SUBLANE, ts + CONV_HALO, W), F32), pltpu.VMEM((M, 2 * W), BF16),
                        pltpu.VMEM((ts, W), BF16), pltpu.VMEM((ts, W), BF16)],
        compiler_params=_params("arbitrary", "arbitrary"),
        name="conv_xattn_merge",
    )(x, y_rwkv, y_mla, mem, *consts)


def _rotate_half_cols(w):
    half = w.shape[-1] // 2
    return jnp.concatenate([-w[..., half:], w[..., :half]], axis=-1)


def kernel(x, mem, positions, w_in, b_gate, rwkv_mu, rwkv_w0, rwkv_w2, rwkv_a0, rwkv_a2, rwkv_k_k, rwkv_k_a, rwkv_r_k, rwkv_lnx_g, rwkv_lnx_b, mla_q_norm, mla_w_uq, mla_kv_norm, mla_w_ukv, conv_w, conv_b, conv_ln_g, conv_ln_b, xattn_w_mem_kv, w_o_branch, w_out, ln_g, ln_b):
    B, S, D = x.shape
    depth = w_in.shape[0]
    W = D // 2
    q_lora, kv_lora = mla_q_norm.shape[-1], mla_kv_norm.shape[-1]
    heads = W // MLA_V
    P = 3 * W + 2 * RWKV_LORA
    alpha = (2.0 * depth) ** 0.25
    assert W % LANE == 0 and LANE == MLA_NOPE + MLA_V and MLA_NOPE + MLA_ROPE <= LANE
    ts = min(S, 256)
    qb = min(S, 256)
    tq = min(S, 8 * qb)
    assert S % ts == 0 and ts % CHUNK == 0 and ts % ATT_TK == 0 and S % tq == 0 and tq % qb == 0 and qb % ATT_TK == 0

    sizes = (P, W, q_lora, kv_lora, MLA_ROPE, W, 2 * W, W, W, W, N_BRANCH * D)
    offs = [0]
    for n in sizes:
        offs.append(offs[-1] + n)
    o_rw, _, o_q, o_kv, o_kpe, o_mg, o_cu, _, o_xq, _, o_merge, o_end = offs
    assert o_end == w_in.shape[-1]

    inv_freq = ROPE_THETA ** (-jnp.arange(0, MLA_ROPE, 2, dtype=F32) / MLA_ROPE)
    zeros_f = lambda n: jnp.zeros((n,), F32)
    freq128 = jnp.concatenate([zeros_f(MLA_NOPE), inv_freq, inv_freq, zeros_f(LANE - MLA_NOPE - MLA_ROPE)])[None, :]
    cos, sin = _rope_tables(positions, freq128)

    lane_head = jnp.arange(MXU_WIDTH) // RWKV_HEAD_DIM
    seg = (lane_head[:, None] == lane_head[None, :]).astype(BF16)
    zl = jnp.zeros((RWKV_LORA, W), F32)
    row = lambda a: a.reshape(1, -1)

    for l in range(depth):
        wi = w_in[l]
        w_rw = wi[:, o_rw:o_q].astype(BF16)
        w2p = jnp.concatenate([rwkv_w2[l], zl], axis=0).astype(BF16)
        a2p = jnp.concatenate([zl, rwkv_a2[l]], axis=0).astype(BF16)
        y_rwkv = _rwkv_branch(x, w_rw, row(rwkv_mu[l]), row(rwkv_w0[l]), w2p, row(rwkv_a0[l]), a2p,
                              row(rwkv_k_k[l]), row(rwkv_k_a[l]), row(rwkv_r_k[l]), row(rwkv_lnx_g[l]),
                              row(rwkv_lnx_b[l]), seg, ts)

        kpe_w = wi[:, o_kpe:o_mg]
        pad_l, pad_r = jnp.zeros((D, MLA_NOPE), F32), jnp.zeros((D, LANE - MLA_NOPE - MLA_ROPE), F32)
        w_mla = jnp.concatenate([wi[:, o_q:o_kpe], pad_l, kpe_w, pad_r, pad_l, _rotate_half_cols(kpe_w), pad_r,
                                 wi[:, o_mg:o_cu]], axis=1).astype(BF16)
        wq = mla_w_uq[l].reshape(q_lora, heads, MLA_NOPE + MLA_ROPE)
        q_nope, q_pe = wq[..., :MLA_NOPE], wq[..., MLA_NOPE:]
        zq = lambda n: jnp.zeros((q_lora, heads, n), F32)
        wq_plain = jnp.concatenate([q_nope, q_pe, zq(LANE - MLA_NOPE - MLA_ROPE)], axis=-1)
        wq_rot = jnp.concatenate([zq(MLA_NOPE), _rotate_half_cols(q_pe), zq(LANE - MLA_NOPE - MLA_ROPE)], axis=-1)
        wq2 = jnp.concatenate([wq_plain.reshape(q_lora, -1), wq_rot.reshape(q_lora, -1)], axis=1).astype(BF16)
        wkv = mla_w_ukv[l].reshape(kv_lora, heads, MLA_NOPE + MLA_V)
        wk = jnp.concatenate([wkv[..., :MLA_NOPE], jnp.zeros((kv_lora, heads, LANE - MLA_NOPE), F32)], axis=-1)
        wv = jnp.concatenate([wkv[..., MLA_NOPE:], jnp.zeros((kv_lora, heads, LANE - MLA_V), F32)], axis=-1)
        wkv2 = jnp.concatenate([wk.reshape(kv_lora, -1), wv.reshape(kv_lora, -1)], axis=1).astype(BF16)
        q, k, vt, g = _mla_prep(x, w_mla, row(mla_q_norm[l]), row(mla_kv_norm[l]), wq2, wkv2, cos, sin, heads, min(S, 2 * ts))
        y_mla = _mla_attention(q, k, vt, g, tq, qb)

        x = _tail(x, y_rwkv, y_mla, mem, wi[:, o_cu:o_xq].astype(BF16), conv_w[l], row(conv_b[l]),
                  row(conv_ln_g[l]), row(conv_ln_b[l]), wi[:, o_xq:o_merge].astype(BF16),
                  xattn_w_mem_kv[l].astype(BF16), wi[:, o_merge:].astype(BF16), b_gate[l],
                  w_o_branch[l].astype(BF16), w_out[l].astype(BF16), row(ln_g[l]), row(ln_b[l]), alpha, min(S, 2 * ts))
    return x
```

```python
import functools

import jax
import jax.numpy as jnp
from jax import lax
from jax.experimental import pallas as pl
from jax.experimental.pallas import tpu as pltpu

F32 = jnp.float32
BF16 = jnp.bfloat16

N_BRANCH = 4
RWKV_HEAD_DIM = 64
RWKV_LORA = 64
RWKV_GN_EPS = 64e-5
MLA_NOPE = 64
MLA_ROPE = 32
MLA_V = 64
ROPE_THETA = 10000.0
CONV_KERNEL = 31
XATTN_HEADS = 4
LN_EPS = 1e-5
RMS_EPS = 1e-6

LANE = 128
SUBLANE = 8
MXU_WIDTH = 256
CHUNK = 64
CONV_HALO = 32
ATT_TK = 256
ATT_VROWS = 80
LOG2_E = 1.4426950408889634
VMEM_LIMIT = 56 * 1024 * 1024


def _dot(a, b):
    return jnp.dot(a.astype(BF16), b.astype(BF16), preferred_element_type=F32)


def _dot_nt(a, b):
    return lax.dot_general(a.astype(BF16), b.astype(BF16), (((1,), (1,)), ((), ())),
                           preferred_element_type=F32)


def _split_hi_lo(a):
    hi = a.astype(BF16)
    lo = (a - hi.astype(F32)).astype(BF16)
    return hi, lo


def _dot_0_1_rhs(a, ones_like_matrix):
    hi, lo = _split_hi_lo(a)
    return (jnp.dot(hi, ones_like_matrix, preferred_element_type=F32)
            + jnp.dot(lo, ones_like_matrix, preferred_element_type=F32))


def _head_sums(a, seg):
    g = seg.shape[0]
    return jnp.concatenate([_dot_0_1_rhs(a[:, o:o + g], seg) for o in range(0, a.shape[1], g)], axis=1)


def _sigmoid(x):
    return 1.0 / (1.0 + jnp.exp(-x))


def _silu(x):
    return x * _sigmoid(x)


def _const_spec(shape, single_buffer=False):
    nd = len(shape)
    mode = pl.Buffered(1) if single_buffer else None
    return pl.BlockSpec(shape, lambda *_: (0,) * nd, pipeline_mode=mode)


def _params(*sem):
    return pltpu.CompilerParams(dimension_semantics=sem, vmem_limit_bytes=VMEM_LIMIT)


def _rope_kernel(pos_ref, f_ref, cos_ref, sin_ref):
    ang = pos_ref[0].astype(F32) * f_ref[...]
    cos_ref[0] = jnp.cos(ang)
    sin_ref[0] = jnp.sin(ang)


def _rope_tables(positions, freq128):
    B, S = positions.shape
    return pl.pallas_call(
        _rope_kernel,
        grid=(B,),
        in_specs=[pl.BlockSpec((1, S, 1), lambda b: (b, 0, 0)), _const_spec((1, LANE))],
        out_specs=[pl.BlockSpec((1, S, LANE), lambda b: (b, 0, 0))] * 2,
        out_shape=[jax.ShapeDtypeStruct((B, S, LANE), F32)] * 2,
        compiler_params=_params("arbitrary"),
        name="rope_tables",
    )(positions[..., None], freq128)


def _rwkv_kernel(x_ref, w_ref, mu_ref, w0_ref, w2_ref, a0_ref, a2_ref, kk_ref, ka_ref, rk_ref,
                 lng_ref, lnb_ref, seg_ref, o_ref,
                 carry_ref, z_ref, r_s, k_s, v_s, g_s, at_s, rt_s, bt_s, kt_s, bh_s, kh_s, et_s, y_s, *, ts, width):
    W = width
    P = 3 * W + 2 * RWKV_LORA
    n_pairs = W // LANE
    s_idx = pl.program_id(1)

    @pl.when(s_idx == 0)
    def _():
        carry_ref[...] = jnp.zeros_like(carry_ref)
        z_ref[...] = jnp.zeros_like(z_ref)

    h = jnp.dot(x_ref[0].astype(BF16), w_ref[...], preferred_element_type=F32)
    hp = h[:, :P]
    rows = lax.broadcasted_iota(jnp.int32, hp.shape, 0)
    prev = jnp.where(rows == 0, carry_ref[0:1, :], pltpu.roll(hp, 1, 0))
    carry_ref[0:1, :] = hp[ts - 1:ts, :]
    p = hp + mu_ref[...] * (prev - hp)
    r, k, v, wa = p[:, :W], p[:, W:2 * W], p[:, 2 * W:3 * W], p[:, 3 * W:]

    zz = -(w0_ref[...] + _dot(jnp.tanh(wa), w2_ref[...]))
    softplus = jnp.maximum(zz, 0.0) + jnp.log(1.0 + jnp.exp(-jnp.abs(zz)))
    lw = -jnp.exp(-softplus - 0.5)
    a = _sigmoid(a0_ref[...] + _dot(wa, a2_ref[...]))
    kk = k * kk_ref[...]
    ss = _head_sums(kk * kk, seg_ref[...])
    kkn = kk * lax.rsqrt(jnp.maximum(ss, 1e-24))
    kmod = k * (1.0 + (a - 1.0) * ka_ref[...])
    b = kkn * a

    C = CHUNK
    n_chunks = ts // C
    rr = lax.broadcasted_iota(jnp.int32, (ts, ts), 0)
    cc = lax.broadcasted_iota(jnp.int32, (ts, ts), 1)
    same_chunk = (rr // C) == (cc // C)
    lw_hi, lw_lo = _split_hi_lo(lw)
    tri = jnp.where(same_chunk & (rr >= cc), 1.0, 0.0).astype(BF16)
    blk = jnp.where(same_chunk, 1.0, 0.0).astype(BF16)
    cs = jnp.dot(tri, lw_hi, preferred_element_type=F32) + jnp.dot(tri, lw_lo, preferred_element_type=F32)
    tot = jnp.dot(blk, lw_hi, preferred_element_type=F32) + jnp.dot(blk, lw_lo, preferred_element_type=F32)
    w_inv, w_end = jnp.exp(-cs), jnp.exp(tot - cs)
    r_s[...] = r
    k_s[...] = kmod
    v_s[...] = v
    g_s[...] = _silu(h[:, P:])
    at_s[...] = -kkn * jnp.exp(cs - lw)
    rt_s[...] = r * jnp.exp(cs)
    bt_s[...] = b * w_inv
    kt_s[...] = kmod * w_inv
    bh_s[...] = b * w_end
    kh_s[...] = kmod * w_end
    et_s[...] = jnp.exp(tot)

    units = [(c, pr) for c in range(n_chunks) for pr in range(n_pairs)]
    first_head = lax.broadcasted_iota(jnp.int32, (C, LANE), 1) < RWKV_HEAD_DIM

    def stacked(ref):
        out = []
        for c, pr in units:
            t = ref[c * C:(c + 1) * C, pr * LANE:(pr + 1) * LANE]
            out.append(jnp.concatenate([jnp.where(first_head, t, 0.0), jnp.where(first_head, 0.0, t)],
                                       axis=0).astype(BF16))
        return jnp.stack(out)

    def bmm(x, y):
        return jnp.einsum('umk,ukn->umn', x.astype(BF16), y.astype(BF16), preferred_element_type=F32)

    def bmm_nt(x, y):
        return jnp.einsum('umk,unk->umn', x.astype(BF16), y.astype(BF16), preferred_element_type=F32)

    def bmm_tn(x, y):
        return jnp.einsum('ukm,ukn->umn', x.astype(BF16), y.astype(BF16), preferred_element_type=F32)

    ri = lax.broadcasted_iota(jnp.int32, (2 * C, 2 * C), 0)
    ci = lax.broadcasted_iota(jnp.int32, (2 * C, 2 * C), 1)
    strict = (ri > ci)[None]
    incl = (ri >= ci)[None]
    eye = (ri == ci).astype(F32)[None]

    As, Rs, Bs, Ks = stacked(at_s), stacked(rt_s), stacked(bt_s), stacked(kt_s)
    Vs, Bh, Kh = stacked(v_s), stacked(bh_s), stacked(kh_s)
    G = bmm_nt(jnp.concatenate([As, Rs], axis=1), jnp.concatenate([Bs, Ks], axis=1))
    L = jnp.where(strict, G[:, :2 * C, :2 * C], 0.0)
    Lak = jnp.where(strict, G[:, :2 * C, 2 * C:], 0.0)
    Mrb = jnp.where(incl, G[:, 2 * C:, :2 * C], 0.0)
    Mrk = jnp.where(incl, G[:, 2 * C:, 2 * C:], 0.0)
    T = eye + L
    Lb = L.astype(BF16)
    Pw = bmm(Lb, Lb)
    step = 2
    while step * 2 < C:
        Pb = Pw.astype(BF16)
        R2 = bmm(Pb, jnp.concatenate([Pb, T.astype(BF16)], axis=2))
        Pw, T = R2[:, :, :2 * C], T + R2[:, :, 2 * C:]
        step *= 2
    T = T + bmm(Pw, T)
    X = bmm(Lak, Vs)
    AU = bmm(T, jnp.concatenate([As, X.astype(BF16)], axis=2))
    Ah, U0 = AU[:, :, :LANE], AU[:, :, LANE:]
    Gc = bmm_tn(Ah, Bh)
    Hc = bmm_tn(jnp.concatenate([U0.astype(BF16), Vs], axis=1), jnp.concatenate([Bh, Kh], axis=1))
    Z = z_ref[...]
    starts = []
    for c in range(n_chunks):
        us = slice(c * n_pairs, (c + 1) * n_pairs)
        e_tot = jnp.stack([et_s[c * C:c * C + 1, pr * LANE:(pr + 1) * LANE] for pr in range(n_pairs)])
        starts.append(Z)
        Z = Z * e_tot + bmm(Z, Gc[us]) + Hc[us]
    z_ref[...] = Z
    Z0 = jnp.concatenate(starts, axis=0)
    AR = bmm_nt(jnp.concatenate([Ah.astype(BF16), Rs], axis=1), Z0)
    Uu = AR[:, :2 * C] + U0
    Ys = AR[:, 2 * C:] + bmm(jnp.concatenate([Mrb.astype(BF16), Mrk.astype(BF16)], axis=2),
                             jnp.concatenate([Uu.astype(BF16), Vs], axis=1))
    ysum = Ys[:, :C] + Ys[:, C:]
    for u, (c, pr) in enumerate(units):
        y_s[c * C:(c + 1) * C, pr * LANE:(pr + 1) * LANE] = ysum[u]

    y = y_s[...]
    inv_n = 1.0 / RWKV_HEAD_DIM
    y_mu = _head_sums(y, seg_ref[...]) * inv_n
    d = y - y_mu
    y_var = _head_sums(d * d, seg_ref[...]) * inv_n
    yn = d * lax.rsqrt(y_var + RWKV_GN_EPS) * lng_ref[...] + lnb_ref[...]
    bonus = _head_sums(r_s[...] * k_s[...] * rk_ref[...], seg_ref[...]) * v_s[...]
    o_ref[0] = ((yn + bonus) * g_s[...]).astype(o_ref.dtype)


def _rwkv_branch(x, w_rw, mu, w0, w2p, a0, a2p, k_k, k_a, r_k, lnx_g, lnx_b, seg, ts):
    B, S, D = x.shape
    W = w0.shape[-1]
    P = 3 * W + 2 * RWKV_LORA
    n_pairs = W // LANE
    row = lambda n: _const_spec((1, n))
    scr = lambda: pltpu.VMEM((ts, W), F32)
    return pl.pallas_call(
        functools.partial(_rwkv_kernel, ts=ts, width=W),
        grid=(B, S // ts),
        in_specs=[pl.BlockSpec((1, ts, D), lambda b, s: (b, s, 0)),
                  _const_spec(w_rw.shape), row(P), row(W), _const_spec(w2p.shape), row(W),
                  _const_spec(a2p.shape), row(W), row(W), row(W), row(W), row(W), _const_spec(seg.shape)],
        out_specs=pl.BlockSpec((1, ts, W), lambda b, s: (b, s, 0)),
        out_shape=jax.ShapeDtypeStruct((B, S, W), BF16),
        scratch_shapes=[pltpu.VMEM((8, P), F32), pltpu.VMEM((n_pairs, LANE, LANE), F32),
                        scr(), scr(), scr(), scr(), scr(), scr(), scr(), scr(), scr(), scr(), scr(), scr()],
        compiler_params=_params("arbitrary", "arbitrary"),
        name="rwkv_branch",
    )(x, w_rw, mu, w0, w2p, a0, a2p, k_k, k_a, r_k, lnx_g, lnx_b, seg)


def _mla_prep_kernel(x_ref, w_ref, qn_ref, kvn_ref, wq_ref, wkv_ref, cos_ref, sin_ref,
                     q_ref, k_ref, vt_ref, g_ref, *, q_lora, kv_lora, heads, scale):
    h = jnp.dot(x_ref[0].astype(BF16), w_ref[...], preferred_element_type=F32)
    ql = h[:, :q_lora]
    kvl = h[:, q_lora:q_lora + kv_lora]
    o = q_lora + kv_lora
    kpe, kpe_rot = h[:, o:o + LANE], h[:, o + LANE:o + 2 * LANE]
    gate = h[:, o + 2 * LANE:]
    cos, sin = cos_ref[0], sin_ref[0]
    qn = ql * lax.rsqrt(jnp.mean(ql * ql, axis=-1, keepdims=True) + RMS_EPS) * qn_ref[...]
    kvn = kvl * lax.rsqrt(jnp.mean(kvl * kvl, axis=-1, keepdims=True) + RMS_EPS) * kvn_ref[...]
    q2 = _dot(qn, wq_ref[...])
    kv = _dot(kvn, wkv_ref[...])
    hw = heads * LANE
    k_pe = kpe * cos + kpe_rot * sin
    ones_lane = jnp.where(lax.broadcasted_iota(jnp.int32, (1, LANE), 1) == MLA_V, 1.0, 0.0)
    ts = h.shape[0]
    for i in range(heads):
        ls = slice(i * LANE, (i + 1) * LANE)
        q_ref[0, :, ls] = ((q2[:, ls] * cos + q2[:, hw + i * LANE:hw + (i + 1) * LANE] * sin) * scale).astype(BF16)
        k_ref[0, :, ls] = (kv[:, ls] + k_pe).astype(BF16)
        v_t = (kv[:, hw + i * LANE:hw + (i + 1) * LANE] + ones_lane).T
        for t in range(ts // ATT_TK):
            vt_ref[0, i, t] = v_t[:ATT_VROWS, t * ATT_TK:(t + 1) * ATT_TK].astype(BF16)
    g_ref[0] = _silu(gate).astype(BF16)


def _mla_prep(x, w_mla, q_norm, kv_norm, wq2, wkv2, cos, sin, heads, ts):
    B, S, D = x.shape
    q_lora, kv_lora = q_norm.shape[-1], kv_norm.shape[-1]
    W = heads * MLA_V
    tile = lambda n: pl.BlockSpec((1, ts, n), lambda b, s: (b, s, 0))
    return pl.pallas_call(
        functools.partial(_mla_prep_kernel, q_lora=q_lora, kv_lora=kv_lora, heads=heads,
                          scale=float(MLA_NOPE + MLA_ROPE) ** -0.5 * LOG2_E),
        grid=(B, S // ts),
        in_specs=[tile(D), _const_spec(w_mla.shape), _const_spec((1, q_lora)), _const_spec((1, kv_lora)),
                  _const_spec(wq2.shape), _const_spec(wkv2.shape), tile(LANE), tile(LANE)],
        out_specs=[tile(heads * LANE), tile(heads * LANE),
                   pl.BlockSpec((1, heads, ts // ATT_TK, ATT_VROWS, ATT_TK), lambda b, s: (b, 0, s, 0, 0)), tile(W)],
        out_shape=[jax.ShapeDtypeStruct((B, S, heads * LANE), BF16)] * 2
        + [jax.ShapeDtypeStruct((B, heads, S // ATT_TK, ATT_VROWS, ATT_TK), BF16), jax.ShapeDtypeStruct((B, S, W), BF16)],
        compiler_params=_params("arbitrary", "arbitrary"),
        name="mla_prep",
    )(x, w_mla, q_norm, kv_norm, wq2, wkv2, cos, sin)


def _mla_attn_kernel(q_ref, k_ref, vt_ref, g_ref, o_ref, m_s, acc_s, s_s, *, tq, qb):
    i = pl.program_id(2)
    tk = ATT_TK
    n_blk = tq // qb
    streams = [(e, hb) for e in range(2) for hb in range(n_blk)]
    key_i = lax.broadcasted_iota(jnp.int32, (tk, qb), 0)
    qry_i = lax.broadcasted_iota(jnp.int32, (tk, qb), 1)

    def scores(j, mode, buf):
        rows_k = pl.ds(pl.multiple_of(j * tk, tk), tk)
        for idx, (e, hb) in enumerate(streams):
            if mode[hb] is not None:
                ls = slice(e * LANE, (e + 1) * LANE)
                s_s[buf, idx] = lax.dot_general(k_ref[0, rows_k, ls], q_ref[0, hb * qb:(hb + 1) * qb, ls],
                                                (((1,), (1,)), ((), ())), preferred_element_type=F32)

    def consume(j, mode, buf):
        for idx, (e, hb) in enumerate(streams):
            if mode[hb] is None:
                continue
            s = s_s[buf, idx]
            if mode[hb] is not False:
                s = jnp.where(key_i + mode[hb] <= qry_i, s, -1e30)
            m = m_s[idx, 0:1, :]
            m_new = jnp.maximum(m, jnp.max(s, axis=0, keepdims=True))
            p = jnp.exp2(s - m_new)
            acc_s[idx] = jnp.exp2(m - m_new) * acc_s[idx] + jnp.dot(vt_ref[0, e, j], p.astype(BF16),
                                                                    preferred_element_type=F32)
            m_s[idx, 0:1, :] = m_new

    tiles_per_blk = qb // tk
    n_diag = tq // tk
    assert n_diag % 2 == 0
    first_diag = i * n_diag
    visible = [False] * n_blk

    def diag_mode(d):
        mode = []
        for hb in range(n_blk):
            rel = d - hb * tiles_per_blk
            mode.append(False if rel < 0 else (None if rel >= tiles_per_blk else rel * tk))
        return mode

    m_s[...] = jnp.full(m_s.shape, -1e30, F32)
    acc_s[...] = jnp.zeros(acc_s.shape, F32)
    scores(0, visible, 0)

    def body(jj, carry):
        for u in range(n_diag):
            scores(n_diag * jj + u + 1, visible, (u + 1) % 2)
            consume(n_diag * jj + u, visible, u % 2)
        return carry

    lax.fori_loop(0, i, body, 0)
    for d in range(n_diag):
        if d + 1 < n_diag:
            scores(first_diag + d + 1, diag_mode(d + 1), (d + 1) % 2)
        consume(first_diag + d, diag_mode(d), d % 2)

    for hb in range(n_blk):
        rows = slice(hb * qb, (hb + 1) * qb)
        a0, a1 = acc_s[streams.index((0, hb))], acc_s[streams.index((1, hb))]
        o_t = jnp.concatenate([a0[:MLA_V] / a0[MLA_V:MLA_V + 1], a1[:MLA_V] / a1[MLA_V:MLA_V + 1]], axis=0)
        o_ref[0, rows, :] = (o_t.T * g_ref[0, rows, :].astype(F32)).astype(o_ref.dtype)


def _mla_attention(q, k, vt, g, tq, qb):
    B, S, HW = q.shape
    W = g.shape[-1]
    n_pairs = W // LANE
    n_streams = 2 * (tq // qb)
    return pl.pallas_call(
        functools.partial(_mla_attn_kernel, tq=tq, qb=qb),
        grid=(B, n_pairs, S // tq),
        in_specs=[pl.BlockSpec((1, tq, 2 * LANE), lambda b, p, i: (b, i, p)),
                  pl.BlockSpec((1, S, 2 * LANE), lambda b, p, i: (b, 0, p)),
                  pl.BlockSpec((1, 2, S // ATT_TK, ATT_VROWS, ATT_TK), lambda b, p, i: (b, p, 0, 0, 0)),
                  pl.BlockSpec((1, tq, LANE), lambda b, p, i: (b, i, p))],
        out_specs=pl.BlockSpec((1, tq, LANE), lambda b, p, i: (b, i, p)),
        out_shape=jax.ShapeDtypeStruct((B, S, W), BF16),
        scratch_shapes=[pltpu.VMEM((n_streams, SUBLANE, qb), F32), pltpu.VMEM((n_streams, ATT_VROWS, qb), F32),
                        pltpu.VMEM((2, n_streams, ATT_TK, qb), F32)],
        compiler_params=_params("arbitrary", "arbitrary", "arbitrary"),
        name="mla_attention",
    )(q, k, vt, g)


def _conv_compute(xb, w_ref, cw_ref, cb_ref, lg_ref, lb_ref, hbuf, y_s, *, ts, width):
    W = width
    s_idx = pl.program_id(1)

    @pl.when(s_idx == 0)
    def _():
        hbuf[0, 0:CONV_HALO, :] = jnp.zeros((CONV_HALO, W), F32)

    @pl.when(s_idx > 0)
    def _():
        hbuf[0, 0:CONV_HALO, :] = hbuf[0, ts:ts + CONV_HALO, :]

    h = jnp.dot(xb, w_ref[...], preferred_element_type=F32)
    hbuf[0, CONV_HALO:CONV_HALO + ts, :] = h[:, :W] * _sigmoid(h[:, W:2 * W])
    shifted_rows = ts + CONV_HALO - SUBLANE
    for r in range(1, SUBLANE):
        hbuf[r, 0:shifted_rows, :] = hbuf[0, r:r + shifted_rows, :]
    first_tap = CONV_HALO - (CONV_KERNEL - 1)
    acc = jnp.zeros((ts, W), F32) + cb_ref[...]
    for j in range(CONV_KERNEL):
        q8, r = divmod(first_tap + j, SUBLANE)
        acc = acc + cw_ref[j:j + 1, :] * hbuf[r, q8 * SUBLANE:q8 * SUBLANE + ts, :]
    mu = jnp.mean(acc, axis=-1, keepdims=True)
    d = acc - mu
    var = jnp.mean(d * d, axis=-1, keepdims=True)
    y = _silu(d * lax.rsqrt(var + LN_EPS) * lg_ref[...] + lb_ref[...])
    y_s[...] = (y * _silu(h[:, 2 * W:])).astype(y_s.dtype)


def _xattn_compute(xb, mem_ref, w_ref, wkv_ref, kv_s, y_s, *, width, scale):
    W = width

    @pl.when(pl.program_id(1) == 0)
    def _():
        kv_s[...] = jnp.dot(mem_ref[0].astype(BF16), wkv_ref[...], preferred_element_type=F32).astype(BF16)

    h = jnp.dot(xb, w_ref[...], preferred_element_type=F32)
    for i in range(XATTN_HEADS):
        ls = slice(i * LANE, (i + 1) * LANE)
        s = _dot_nt(h[:, ls] * scale, kv_s[:, ls])
        p = jnp.exp(s - jnp.max(s, axis=-1, keepdims=True))
        o = _dot(p, kv_s[:, W + i * LANE:W + (i + 1) * LANE]) / jnp.sum(p, axis=-1, keepdims=True)
        y_s[:, ls] = (o * _silu(h[:, W + i * LANE:W + (i + 1) * LANE])).astype(y_s.dtype)


def _tail_kernel(x_ref, y0_ref, y1_ref, mem_ref, wc_ref, cw_ref, cb_ref, clg_ref, clb_ref, wx_ref, wkv_ref,
                 wm_ref, bg_ref, wo_ref, wout_ref, lg_ref, lb_ref, o_ref, hbuf, kv_s, yc_s, ym_s,
                 *, alpha, ts, width):
    x = x_ref[0]
    xb = x.astype(BF16)
    D = x.shape[-1]
    _conv_compute(xb, wc_ref, cw_ref, cb_ref, clg_ref, clb_ref, hbuf, yc_s, ts=ts, width=width)
    _xattn_compute(xb, mem_ref, wx_ref, wkv_ref, kv_s, ym_s, width=width,
                   scale=float(width // XATTN_HEADS) ** -0.5)
    merged = None
    for n, y in enumerate((y0_ref[0], y1_ref[0], yc_s[...], ym_s[...])):
        gate = _sigmoid(jnp.dot(xb, wm_ref[:, n * D:(n + 1) * D], preferred_element_type=F32) + bg_ref[n:n + 1, :])
        term = gate * jnp.dot(y, wo_ref[n], preferred_element_type=F32)
        merged = term if merged is None else merged + term
    z = alpha * x + _dot(merged, wout_ref[...])
    mu = jnp.mean(z, axis=-1, keepdims=True)
    d = z - mu
    var = jnp.mean(d * d, axis=-1, keepdims=True)
    o_ref[0] = d * lax.rsqrt(var + LN_EPS) * lg_ref[...] + lb_ref[...]


def _tail(x, y_rwkv, y_mla, mem, w_conv, conv_w, conv_b, conv_ln_g, conv_ln_b, w_x, w_mem_kv,
          w_merge, b_gate, w_o, w_out, ln_g, ln_b, alpha, ts):
    B, S, D = x.shape
    M = mem.shape[1]
    W = y_rwkv.shape[-1]
    consts = (w_conv, conv_w, conv_b, conv_ln_g, conv_ln_b, w_x, w_mem_kv, w_merge, b_gate, w_o, w_out, ln_g, ln_b)
    tile = lambda n: pl.BlockSpec((1, ts, n), lambda b, s: (b, s, 0))
    return pl.pallas_call(
        functools.partial(_tail_kernel, alpha=alpha, ts=ts, width=W),
        grid=(B, S // ts),
        in_specs=[tile(D), tile(W), tile(W), pl.BlockSpec((1, M, D), lambda b, s: (b, 0, 0))]
        + [_const_spec(a.shape, single_buffer=True) for a in consts],
        out_specs=tile(D),
        out_shape=jax.ShapeDtypeStruct((B, S, D), F32),
        scratch_shapes=[pltpu.VMEM((SUBLANE, ts + CONV_HALO, W), F32), pltpu.VMEM((M, 2 * W), BF16),
                        pltpu.VMEM((ts, W), BF16), pltpu.VMEM((ts, W), BF16)],
        compiler_params=_params("arbitrary", "arbitrary"),
        name="conv_xattn_merge",
    )(x, y_rwkv, y_mla, mem, *consts)


def _rotate_half_cols(w):
    half = w.shape[-1] // 2
    return jnp.concatenate([-w[..., half:], w[..., :half]], axis=-1)


def kernel(x, mem, positions, w_in, b_gate, rwkv_mu, rwkv_w0, rwkv_w2, rwkv_a0, rwkv_a2, rwkv_k_k, rwkv_k_a, rwkv_r_k, rwkv_lnx_g, rwkv_lnx_b, mla_q_norm, mla_w_uq, mla_kv_norm, mla_w_ukv, conv_w, conv_b, conv_ln_g, conv_ln_b, xattn_w_mem_kv, w_o_branch, w_out, ln_g, ln_b):
    B, S, D = x.shape
    depth = w_in.shape[0]
    W = D // 2
    q_lora, kv_lora = mla_q_norm.shape[-1], mla_kv_norm.shape[-1]
    heads = W // MLA_V
    P = 3 * W + 2 * RWKV_LORA
    alpha = (2.0 * depth) ** 0.25
    assert W % LANE == 0 and LANE == MLA_NOPE + MLA_V and MLA_NOPE + MLA_ROPE <= LANE
    ts = min(S, 256)
    qb = min(S, 256)
    tq = min(S, 8 * qb)
    assert S % ts == 0 and ts % CHUNK == 0 and ts % ATT_TK == 0 and S % tq == 0 and tq % qb == 0 and qb % ATT_TK == 0

    sizes = (P, W, q_lora, kv_lora, MLA_ROPE, W, 2 * W, W, W, W, N_BRANCH * D)
    offs = [0]
    for n in sizes:
        offs.append(offs[-1] + n)
    o_rw, _, o_q, o_kv, o_kpe, o_mg, o_cu, _, o_xq, _, o_merge, o_end = offs
    assert o_end == w_in.shape[-1]

    inv_freq = ROPE_THETA ** (-jnp.arange(0, MLA_ROPE, 2, dtype=F32) / MLA_ROPE)
    zeros_f = lambda n: jnp.zeros((n,), F32)
    freq128 = jnp.concatenate([zeros_f(MLA_NOPE), inv_freq, inv_freq, zeros_f(LANE - MLA_NOPE - MLA_ROPE)])[None, :]
    cos, sin = _rope_tables(positions, freq128)

    lane_head = jnp.arange(MXU_WIDTH) // RWKV_HEAD_DIM
    seg = (lane_head[:, None] == lane_head[None, :]).astype(BF16)
    zl = jnp.zeros((RWKV_LORA, W), F32)
    row = lambda a: a.reshape(1, -1)

    for l in range(depth):
        wi = w_in[l]
        w_rw = wi[:, o_rw:o_q].astype(BF16)
        w2p = jnp.concatenate([rwkv_w2[l], zl], axis=0).astype(BF16)
        a2p = jnp.concatenate([zl, rwkv_a2[l]], axis=0).astype(BF16)
        y_rwkv = _rwkv_branch(x, w_rw, row(rwkv_mu[l]), row(rwkv_w0[l]), w2p, row(rwkv_a0[l]), a2p,
                              row(rwkv_k_k[l]), row(rwkv_k_a[l]), row(rwkv_r_k[l]), row(rwkv_lnx_g[l]),
                              row(rwkv_lnx_b[l]), seg, ts)

        kpe_w = wi[:, o_kpe:o_mg]
        pad_l, pad_r = jnp.zeros((D, MLA_NOPE), F32), jnp.zeros((D, LANE - MLA_NOPE - MLA_ROPE), F32)
        w_mla = jnp.concatenate([wi[:, o_q:o_kpe], pad_l, kpe_w, pad_r, pad_l, _rotate_half_cols(kpe_w), pad_r,
                                 wi[:, o_mg:o_cu]], axis=1).astype(BF16)
        wq = mla_w_uq[l].reshape(q_lora, heads, MLA_NOPE + MLA_ROPE)
        q_nope, q_pe = wq[..., :MLA_NOPE], wq[..., MLA_NOPE:]
        zq = lambda n: jnp.zeros((q_lora, heads, n), F32)
        wq_plain = jnp.concatenate([q_nope, q_pe, zq(LANE - MLA_NOPE - MLA_ROPE)], axis=-1)
        wq_rot = jnp.concatenate([zq(MLA_NOPE), _rotate_half_cols(q_pe), zq(LANE - MLA_NOPE - MLA_ROPE)], axis=-1)
        wq2 = jnp.concatenate([wq_plain.reshape(q_lora, -1), wq_rot.reshape(q_lora, -1)], axis=1).astype(BF16)
        wkv = mla_w_ukv[l].reshape(kv_lora, heads, MLA_NOPE + MLA_V)
        wk = jnp.concatenate([wkv[..., :MLA_NOPE], jnp.zeros((kv_lora, heads, LANE - MLA_NOPE), F32)], axis=-1)
        wv = jnp.concatenate([wkv[..., MLA_NOPE:], jnp.zeros((kv_lora, heads, LANE - MLA_V), F32)], axis=-1)
        wkv2 = jnp.concatenate([wk.reshape(kv_lora, -1), wv.reshape(kv_lora, -1)], axis=1).astype(BF16)
        q, k, vt, g = _mla_prep(x, w_mla, row(mla_q_norm[l]), row(mla_kv_norm[l]), wq2, wkv2, cos, sin, heads, min(S, 4 * ts))
        y_mla = _mla_attention(q, k, vt, g, tq, qb)

        x = _tail(x, y_rwkv, y_mla, mem, wi[:, o_cu:o_xq].astype(BF16), conv_w[l], row(conv_b[l]),
                  row(conv_ln_g[l]), row(conv_ln_b[l]), wi[:, o_xq:o_merge].astype(BF16),
                  xattn_w_mem_kv[l].astype(BF16), wi[:, o_merge:].astype(BF16), b_gate[l],
                  w_o_branch[l].astype(BF16), w_out[l].astype(BF16), row(ln_g[l]), row(ln_b[l]), alpha, min(S, 2 * ts))
    return x
```
